```python
import math
import jax
import jax.numpy as jnp
from jax import lax
import numpy as np

D_MODEL = 2048
BATCH = 4
SEQ = 2048
DEPTH = 2
DEC_BATCH = 32
DEC_SEQ = 8
PAST_LEN = 8192
PAGE_SIZE = 128

N_EVEN = (DEPTH + 1) // 2
N_ODD = DEPTH // 2
H_A = 8
DK_A = 128
DV_A = 128
A_QK = H_A * DK_A
A_V = H_A * DV_A
A_CONV = 2 * A_QK + A_V
SC_W = 4
DELTA_CHUNK = 64
H_B = 8
DH_B = 128
B_W = H_B * DH_B
H_C = 8
DQ_C = 64
DV_C = 128
C_QK = H_C * 2 * DQ_C
C_V = H_C * DV_C
H_D = 8
DH_D = 128
D_W = H_D * DH_D
EVEN_IN = A_CONV + A_V + 2 * H_A + 3 * B_W + H_B
ODD_IN = 2 * C_QK + C_V + 3 * D_W
D_FF = 5632
FFN_CONV_W = 3
REL_BUCKETS = 32
REL_MAX_DIST = 128
Q_BLOCK = 128
EPS = 1e-6
STATE_KEYS = ('k_fox', 'v_fox', 'logf_fox', 'delta', 'conv_delta', 'k_diff', 'v_diff', 'k_sb', 'v_sb', 'conv_ffn')

kernel_name = 'hybrid_delta_fox_diff_stickbreak_step'


def rms_norm(x, g):
    x32 = x.astype(jnp.float32)
    y = x32 * lax.rsqrt(jnp.mean(x32 * x32, axis=-1, keepdims=True) + EPS)
    return (y * g.astype(jnp.float32)).astype(x.dtype)


def l2_norm(x):
    return x * lax.rsqrt(jnp.sum(x * x, axis=-1, keepdims=True) + EPS)


def split_cols(x, sizes):
    return jnp.split(x, np.cumsum(sizes)[:-1].tolist(), axis=-1)


def causal_depthwise_conv(x, buf, w):
    width, t = w.shape[0], x.shape[1]
    xc = jnp.concatenate([buf.astype(x.dtype), x], axis=1)
    y = xc[:, 0:t] * w[0]
    for i in range(1, width):
        y = y + xc[:, i:i + t] * w[i]
    return y, xc[:, xc.shape[1] - (width - 1):]


def prepend_past(rows, pool, layer, page_table):
    if pool is None:
        return rows
    g = pool[layer, page_table]
    g = g.reshape((g.shape[0], g.shape[1] * g.shape[2]) + g.shape[3:])
    return jnp.concatenate([g.astype(rows.dtype), rows], axis=1)


def sweep_query_blocks(fn, *q_args):
    t = q_args[0].shape[1]
    if t <= Q_BLOCK:
        return fn(*q_args)
    nb = t // Q_BLOCK
    blocks = tuple(jnp.moveaxis(a.reshape((a.shape[0], nb, Q_BLOCK) + a.shape[2:]), 1, 0) for a in q_args)
    out = jnp.moveaxis(lax.map(lambda a: fn(*a), blocks), 0, 1)
    return out.reshape((out.shape[0], t) + out.shape[3:])


def t5_bucket(dist):
    exact = REL_BUCKETS // 2
    d = jnp.maximum(dist, 0)
    far = exact + (jnp.log(jnp.maximum(d, 1).astype(jnp.float32) / exact)
                   / math.log(REL_MAX_DIST / exact) * (REL_BUCKETS - exact)).astype(jnp.int32)
    return jnp.where(d < exact, d, jnp.minimum(far, REL_BUCKETS - 1))


def gated_delta_rule(q, k, v, beta, g, s0):
    b, t, h, dk = q.shape
    dv = v.shape[-1]
    L = math.gcd(t, DELTA_CHUNK)
    nc = t // L

    def chunks(a):
        a = a.reshape((b, nc, L, h) + a.shape[3:])
        return jnp.moveaxis(jnp.moveaxis(a, 3, 2), 1, 0)

    qc, kc, vc, bc, gc = chunks(q), chunks(k), chunks(v), chunks(beta), chunks(g)
    G = jnp.cumsum(gc, axis=-1)
    tri = jnp.tril(jnp.ones((L, L), bool))
    strict = jnp.tril(jnp.ones((L, L), bool), -1)
    decay = jnp.exp(jnp.where(tri, G[..., :, None] - G[..., None, :], -jnp.inf))
    kb = kc * bc[..., None]
    A = jnp.where(strict, jnp.einsum('nbhid,nbhjd->nbhij', kb, kc) * decay, 0.0)
    M = A + jnp.eye(L, dtype=jnp.float32)
    rhs = jnp.concatenate([vc * bc[..., None], kb * jnp.exp(G)[..., None]], axis=-1)
    sol = lax.linalg.triangular_solve(M, rhs, left_side=True, lower=True, unit_diagonal=True)
    u0, wk = sol[..., :dv], sol[..., dv:]
    qk = jnp.einsum('nbhid,nbhjd->nbhij', qc, kc) * decay

    def step(S, inp):
        q_, k_, u_, w_, qk_, G_ = inp
        v_new = u_ - jnp.einsum('bhld,bhdv->bhlv', w_, S)
        o = (jnp.einsum('bhld,bhdv->bhlv', q_ * jnp.exp(G_)[..., None], S)
             + jnp.einsum('bhij,bhjv->bhiv', qk_, v_new))
        G_last = G_[..., -1:]
        S = S * jnp.exp(G_last)[..., None] + jnp.einsum(
            'bhld,bhlv->bhdv', k_ * jnp.exp(G_last - G_)[..., None], v_new)
        return S, o

    S, o = lax.scan(step, s0, (qc, kc, u0, wk, qk, G))
    o = jnp.swapaxes(jnp.moveaxis(o, 0, 1), 2, 3).reshape(b, t, h, dv)
    return o, S


def fox_attend(q, cq, qpos, k, v, ck, kpos):
    s = jnp.einsum('bqhd,bkhd->bhqk', q, k).astype(jnp.float32) * (q.shape[-1] ** -0.5)
    s = s + jnp.swapaxes(cq, 1, 2)[..., None] - jnp.swapaxes(ck, 1, 2)[:, :, None, :]
    mask = kpos[None, None, None, :] <= qpos[:, None, :, None]
    p = jax.nn.softmax(jnp.where(mask, s, -jnp.inf), axis=-1)
    return jnp.einsum('bhqk,bkhd->bqhd', p.astype(v.dtype), v)


def diff_attend(q, qpos, k, v, kpos, rel_bias, lam):
    s = jnp.einsum('bqhmd,bkhmd->bhmqk', q, k).astype(jnp.float32) * (q.shape[-1] ** -0.5)
    dist = qpos[:, :, None] - kpos[None, None, :]
    bias = jnp.moveaxis(rel_bias[t5_bucket(dist)], -1, 1).astype(jnp.float32)
    s = s + bias[:, :, None]
    mask = (dist >= 0)[:, None, None]
    p = jax.nn.softmax(jnp.where(mask, s, -jnp.inf), axis=-1)
    a = p[:, :, 0] - lam * p[:, :, 1]
    return jnp.einsum('bhqk,bkhd->bqhd', a.astype(v.dtype), v)


def sb_attend(q, qpos, k, v, kpos):
    z = jnp.einsum('bqhd,bkhd->bhqk', q, k).astype(jnp.float32) * (q.shape[-1] ** -0.5)
    mask = (kpos[None, None, :] < qpos[:, :, None])[:, None]
    log_keep = jnp.where(mask, jax.nn.log_sigmoid(-z), 0.0)
    later = lax.cumsum(log_keep, axis=3, reverse=True) - log_keep
    a = jnp.where(mask, jnp.exp(jax.nn.log_sigmoid(z) + later), 0.0)
    return jnp.einsum('bhqk,bkhd->bqhd', a.astype(v.dtype), v)


def even_mixer(h, qpos, w_in, conv_w, a_log, dt_bias, onorm_g, qn_g, kn_g, f_bias, w_out,
               conv_buf, s0, fox_pools, layer, page_table):
    b, t, _ = h.shape
    f32 = jnp.float32
    qkv_a, z_a, b_a, a_a, q_b, k_b, v_b, f_b = split_cols(
        h @ w_in, [A_CONV, A_V, H_A, H_A, B_W, B_W, B_W, H_B])
    qkv_a, new_buf = causal_depthwise_conv(qkv_a, conv_buf, conv_w)
    q_a, k_a, v_a = split_cols(jax.nn.silu(qkv_a).astype(f32), [A_QK, A_QK, A_V])
    q_a = l2_norm(q_a.reshape(b, t, H_A, DK_A)) * (DK_A ** -0.5)
    k_a = l2_norm(k_a.reshape(b, t, H_A, DK_A))
    v_a = v_a.reshape(b, t, H_A, DV_A)
    beta = jax.nn.sigmoid(b_a.astype(f32))
    g = -jnp.exp(a_log.astype(f32)) * jax.nn.softplus(a_a.astype(f32) + dt_bias.astype(f32))
    o_a, s_new = gated_delta_rule(q_a, k_a, v_a, beta, g, s0.astype(f32))
    o_a = rms_norm(o_a.astype(h.dtype), onorm_g) * jax.nn.silu(z_a.reshape(b, t, H_A, DV_A))
    q_b = rms_norm(q_b.reshape(b, t, H_B, DH_B), qn_g)
    k_b = rms_norm(k_b.reshape(b, t, H_B, DH_B), kn_g)
    v_b = v_b.reshape(b, t, H_B, DH_B)
    logf = jax.nn.log_sigmoid(f_b.astype(f32) + f_bias.astype(f32))
    pk, pv, pl = fox_pools
    k_all = prepend_past(k_b, pk, layer, page_table)
    v_all = prepend_past(v_b, pv, layer, page_table)
    c_all = jnp.cumsum(prepend_past(logf, pl, layer, page_table), axis=1)
    n_keys = k_all.shape[1]
    kpos = jnp.arange(n_keys, dtype=jnp.int32)
    o_b = sweep_query_blocks(lambda q, cq, qp: fox_attend(q, cq, qp, k_all, v_all, c_all, kpos),
                             q_b, c_all[:, n_keys - t:], qpos)
    out = jnp.concatenate([o_a.reshape(b, t, A_V), o_b.reshape(b, t, B_W)], axis=-1) @ w_out
    return out, (k_b, v_b, logf, s_new, new_buf)


def odd_mixer(h, qpos, w_in, qn_g, kn_g, lam_p, lam_init, subln_g, rel_bias, w_out, pools, layer, page_table):
    b, t, _ = h.shape
    q_c, k_c, v_c, q_d, k_d, v_d = split_cols(h @ w_in, [C_QK, C_QK, C_V, D_W, D_W, D_W])
    q_c = rms_norm(q_c.reshape(b, t, H_C, 2, DQ_C), qn_g)
    k_c = rms_norm(k_c.reshape(b, t, H_C, 2, DQ_C), kn_g).reshape(b, t, H_C, 2 * DQ_C)
    v_c = v_c.reshape(b, t, H_C, DV_C)
    lam32 = lam_p.astype(jnp.float32)
    lam = jnp.exp(jnp.sum(lam32[0] * lam32[1])) - jnp.exp(jnp.sum(lam32[2] * lam32[3])) + lam_init
    q_d = q_d.reshape(b, t, H_D, DH_D)
    k_d = k_d.reshape(b, t, H_D, DH_D)
    v_d = v_d.reshape(b, t, H_D, DH_D)
    pkc, pvc, pkd, pvd = pools
    kc_all = prepend_past(k_c, pkc, layer, page_table)
    n_keys = kc_all.shape[1]
    kc_all = kc_all.reshape(b, n_keys, H_C, 2, DQ_C)
    vc_all = prepend_past(v_c, pvc, layer, page_table)
    kpos = jnp.arange(n_keys, dtype=jnp.int32)
    o_c = sweep_query_blocks(lambda q, qp: diff_attend(q, qp, kc_all, vc_all, kpos, rel_bias, lam), q_c, qpos)
    o_c = rms_norm(o_c, subln_g) * (1.0 - lam_init)
    kd_all = prepend_past(k_d, pkd, layer, page_table)
    vd_all = prepend_past(v_d, pvd, layer, page_table)
    o_d = sweep_query_blocks(lambda q, qp: sb_attend(q, qp, kd_all, vd_all, kpos), q_d, qpos)
    out = jnp.concatenate([o_c.reshape(b, t, C_V), o_d.reshape(b, t, D_W)], axis=-1) @ w_out
    return out, (k_c, v_c, k_d, v_d)


def conv_ffn(h, w_up, conv_w, conv_b, w_down, buf):
    u, new_buf = causal_depthwise_conv(h @ w_up, buf, conv_w)
    gate, up = jnp.split(u + conv_b, 2, axis=-1)
    return (jax.nn.silu(gate) * up) @ w_down, new_buf


def trunk(x, c, qpos, w, past):
    b = x.shape[0]
    fresh = past is None
    pt = None if fresh else past['page_table']
    new = {name: [] for name in STATE_KEYS}
    for l in range(DEPTH):
        i = l // 2
        mod = (jax.nn.silu(c) @ w['w_ada'][l] + w['b_ada'][l]).reshape(b, 6, 1, D_MODEL)
        h = rms_norm(x, w['norm_g'][l, 0]) * (1.0 + mod[:, 1]) + mod[:, 0]
        if l % 2 == 0:
            if fresh:
                conv_buf = jnp.zeros((b, SC_W - 1, A_CONV), x.dtype)
                s0 = jnp.zeros((b, H_A, DK_A, DV_A), jnp.float32)
                pools = (None, None, None)
            else:
                conv_buf = past['state_conv_delta'][i]
                s0 = past['state_delta'][i]
                pools = (past['cache_k_fox'], past['cache_v_fox'], past['cache_logf_fox'])
            m, rows = even_mixer(h, qpos, w['w_in_even'][i], w['delta_conv_w'][i], w['delta_a_log'][i],
                                 w['delta_dt_bias'][i], w['delta_onorm_g'][i], w['fox_qn_g'][i],
                                 w['fox_kn_g'][i], w['fox_f_bias'][i], w['w_out_even'][i],
                                 conv_buf, s0, pools, i, pt)
            for name, r in zip(STATE_KEYS[:5], rows):
                new[name].append(r)
        else:
            if fresh:
                pools = (None, None, None, None)
            else:
                pools = (past['cache_k_diff'], past['cache_v_diff'], past['cache_k_sb'], past['cache_v_sb'])
            lam_init = 0.8 - 0.6 * math.exp(-0.3 * l)
            m, rows = odd_mixer(h, qpos, w['w_in_odd'][i], w['diff_qn_g'][i], w['diff_kn_g'][i],
                                w['diff_lambda'][i], lam_init, w['diff_subln_g'][i], w['rel_bias'],
                                w['w_out_odd'][i], pools, i, pt)
            for name, r in zip(STATE_KEYS[5:9], rows):
                new[name].append(r)
        x = x + mod[:, 2] * m
        h = rms_norm(x, w['norm_g'][l, 1]) * (1.0 + mod[:, 4]) + mod[:, 3]
        if fresh:
            ffn_buf = jnp.zeros((b, FFN_CONV_W - 1, 2 * D_FF), x.dtype)
        else:
            ffn_buf = past['state_conv_ffn'][l]
        f, fb = conv_ffn(h, w['ffn_w_up'][l], w['ffn_conv_w'][l], w['ffn_conv_b'][l], w['ffn_w_down'][l], ffn_buf)
        new['conv_ffn'].append(fb)
        x = x + mod[:, 5] * f
    return x, {name: jnp.stack(v) for name, v in new.items()}


def setup_inputs(seed: int = 0) -> dict:
    key = jax.random.key(seed)
    ks = jax.random.split(key, 48)

    def nrm(i, shape, scale=1.0):
        return jax.random.normal(ks[i], shape, jnp.float32) * scale

    n_pages = PAST_LEN // PAGE_SIZE
    n_used = DEC_BATCH * n_pages
    n_pool = n_used + max(1, n_used // 4)
    page_table = jax.random.permutation(ks[0], n_pool)[:n_used].reshape(DEC_BATCH, n_pages).astype(jnp.int32)
    dt = jnp.exp(jax.random.uniform(ks[1], (N_EVEN, H_A), jnp.float32, math.log(1e-3), math.log(1e-1)))
    return {
        'x_prompt': nrm(2, (BATCH, SEQ, D_MODEL)),
        'x_sample': nrm(3, (DEC_BATCH, DEC_SEQ, D_MODEL)),
        'cache_k_fox': nrm(4, (N_EVEN, n_pool, PAGE_SIZE, H_B, DH_B)),
        'cache_v_fox': nrm(5, (N_EVEN, n_pool, PAGE_SIZE, H_B, DH_B)),
        'cache_logf_fox': jax.nn.log_sigmoid(nrm(6, (N_EVEN, n_pool, PAGE_SIZE, H_B)) + 2.5),
        'cache_k_diff': nrm(7, (N_ODD, n_pool, PAGE_SIZE, H_C, 2 * DQ_C)),
        'cache_v_diff': nrm(8, (N_ODD, n_pool, PAGE_SIZE, H_C, DV_C)),
        'cache_k_sb': nrm(9, (N_ODD, n_pool, PAGE_SIZE, H_D, DH_D)),
        'cache_v_sb': nrm(10, (N_ODD, n_pool, PAGE_SIZE, H_D, DH_D)),
        'state_delta': nrm(11, (N_EVEN, DEC_BATCH, H_A, DK_A, DV_A), DK_A ** -0.5),
        'state_conv_delta': nrm(12, (N_EVEN, DEC_BATCH, SC_W - 1, A_CONV)),
        'state_conv_ffn': nrm(13, (DEPTH, DEC_BATCH, FFN_CONV_W - 1, 2 * D_FF)),
        'page_table': page_table,
        'c_prompt': nrm(14, (BATCH, D_MODEL)),
        'c_sample': nrm(15, (DEC_BATCH, D_MODEL)),
        'w_ada': nrm(16, (DEPTH, D_MODEL, 6 * D_MODEL), 0.5 * D_MODEL ** -0.5),
        'b_ada': nrm(17, (DEPTH, 6 * D_MODEL), 0.02),
        'norm_g': 1.0 + nrm(18, (DEPTH, 2, D_MODEL), 0.02),
        'w_in_even': nrm(19, (N_EVEN, D_MODEL, EVEN_IN), D_MODEL ** -0.5),
        'w_out_even': nrm(20, (N_EVEN, A_V + B_W, D_MODEL), (A_V + B_W) ** -0.5),
        'delta_conv_w': nrm(21, (N_EVEN, SC_W, A_CONV), SC_W ** -0.5),
        'delta_a_log': jnp.log(jax.random.uniform(ks[22], (N_EVEN, H_A), jnp.float32, 1.0, 16.0)),
        'delta_dt_bias': dt + jnp.log(-jnp.expm1(-dt)),
        'delta_onorm_g': 1.0 + nrm(23, (N_EVEN, DV_A), 0.02),
        'fox_qn_g': 1.0 + nrm(24, (N_EVEN, DH_B), 0.02),
        'fox_kn_g': 1.0 + nrm(25, (N_EVEN, DH_B), 0.02),
        'fox_f_bias': jax.random.uniform(ks[26], (N_EVEN, H_B), jnp.float32, 1.0, 4.0),
        'w_in_odd': nrm(27, (N_ODD, D_MODEL, ODD_IN), D_MODEL ** -0.5),
        'w_out_odd': nrm(28, (N_ODD, C_V + D_W, D_MODEL), (C_V + D_W) ** -0.5),
        'diff_qn_g': 1.0 + nrm(29, (N_ODD, DQ_C), 0.02),
        'diff_kn_g': 1.0 + nrm(30, (N_ODD, DQ_C), 0.02),
        'diff_lambda': nrm(31, (N_ODD, 4, DQ_C), 0.1),
        'diff_subln_g': 1.0 + nrm(32, (N_ODD, DV_C), 0.02),
        'rel_bias': nrm(33, (REL_BUCKETS, H_C), 0.5),
        'ffn_w_up': nrm(34, (DEPTH, D_MODEL, 2 * D_FF), D_MODEL ** -0.5),
        'ffn_conv_w': nrm(35, (DEPTH, FFN_CONV_W, 2 * D_FF), FFN_CONV_W ** -0.5),
        'ffn_conv_b': nrm(36, (DEPTH, 2 * D_FF), 0.02),
        'ffn_w_down': nrm(37, (DEPTH, D_FF, D_MODEL), D_FF ** -0.5),
    }


def reference(x_prompt, x_sample, cache_k_fox, cache_v_fox, cache_logf_fox, cache_k_diff, cache_v_diff,
              cache_k_sb, cache_v_sb, state_delta, state_conv_delta, state_conv_ffn, page_table,
              c_prompt, c_sample, w_ada, b_ada, norm_g, w_in_even, w_out_even, delta_conv_w, delta_a_log,
              delta_dt_bias, delta_onorm_g, fox_qn_g, fox_kn_g, fox_f_bias, w_in_odd, w_out_odd, diff_qn_g,
              diff_kn_g, diff_lambda, diff_subln_g, rel_bias, ffn_w_up, ffn_conv_w, ffn_conv_b, ffn_w_down):
    weights = {
        'w_ada': w_ada, 'b_ada': b_ada, 'norm_g': norm_g,
        'w_in_even': w_in_even, 'w_out_even': w_out_even, 'delta_conv_w': delta_conv_w,
        'delta_a_log': delta_a_log, 'delta_dt_bias': delta_dt_bias, 'delta_onorm_g': delta_onorm_g,
        'fox_qn_g': fox_qn_g, 'fox_kn_g': fox_kn_g, 'fox_f_bias': fox_f_bias,
        'w_in_odd': w_in_odd, 'w_out_odd': w_out_odd, 'diff_qn_g': diff_qn_g, 'diff_kn_g': diff_kn_g,
        'diff_lambda': diff_lambda, 'diff_subln_g': diff_subln_g, 'rel_bias': rel_bias,
        'ffn_w_up': ffn_w_up, 'ffn_conv_w': ffn_conv_w, 'ffn_conv_b': ffn_conv_b, 'ffn_w_down': ffn_w_down,
    }
    past = {
        'cache_k_fox': cache_k_fox, 'cache_v_fox': cache_v_fox, 'cache_logf_fox': cache_logf_fox,
        'cache_k_diff': cache_k_diff, 'cache_v_diff': cache_v_diff, 'cache_k_sb': cache_k_sb,
        'cache_v_sb': cache_v_sb, 'state_delta': state_delta, 'state_conv_delta': state_conv_delta,
        'state_conv_ffn': state_conv_ffn, 'page_table': page_table,
    }
    qpos_p = jnp.broadcast_to(jnp.arange(x_prompt.shape[1], dtype=jnp.int32), x_prompt.shape[:2])
    past_len = page_table.shape[1] * cache_k_fox.shape[2]
    qpos_s = jnp.broadcast_to(past_len + jnp.arange(x_sample.shape[1], dtype=jnp.int32), x_sample.shape[:2])
    y_prompt, sp = trunk(x_prompt, c_prompt, qpos_p, weights, None)
    y_sample, ss = trunk(x_sample, c_sample, qpos_s, weights, past)
    return (y_prompt, y_sample,
            sp['k_fox'], sp['v_fox'], sp['logf_fox'], sp['delta'], sp['conv_delta'],
            sp['k_diff'], sp['v_diff'], sp['k_sb'], sp['v_sb'], sp['conv_ffn'],
            ss['k_fox'], ss['v_fox'], ss['logf_fox'], ss['delta'], ss['conv_delta'],
            ss['k_diff'], ss['v_diff'], ss['k_sb'], ss['v_sb'], ss['conv_ffn'])
```

```python
import functools
import math

import numpy as np
import jax
import jax.numpy as jnp
from jax import lax
from jax.experimental import pallas as pl
from jax.experimental.pallas import tpu as pltpu

F32 = jnp.float32
BF16 = jnp.bfloat16
HI = lax.Precision.HIGHEST

LANES = 128
SUBLANES = 8
MIB = 1024 * 1024

EPS = 1e-6
H = 8
DH = 128
DQ_C = 64
SC_W = 4
FFN_CONV_W = 3
DELTA_CHUNK = 64
REL_BUCKETS = 32
REL_MAX_DIST = 128
PAGE = 128

A_CONV = 3 * H * DH
QKV_A, Z_A, Q_B, K_B, V_B, SMALL = 0, 3072, 4096, 5120, 6144, 7168
EVEN_N = 7680
BETA_L, G_L, LOGF_L = 0, 8, 16


def _t5_bucket_starts():
    exact = REL_BUCKETS // 2
    d = np.arange(REL_MAX_DIST + 1)
    far = exact + (np.log(np.maximum(d, 1).astype(np.float32) / exact)
                   / math.log(REL_MAX_DIST / exact) * (REL_BUCKETS - exact)).astype(np.int32)
    bucket = np.where(d < exact, d, np.minimum(far, REL_BUCKETS - 1))
    return [int(np.argmax(bucket == b)) for b in range(REL_BUCKETS)]


T5_STARTS = _t5_bucket_starts()


def _cparams(n_axes, vmem_mib):
    return pltpu.CompilerParams(dimension_semantics=("arbitrary",) * n_axes,
                                vmem_limit_bytes=vmem_mib * MIB)


def _nt(a, b, precision=None):
    return lax.dot_general(a, b, (((1,), (1,)), ((), ())), precision=precision,
                           preferred_element_type=F32)


def _softplus_neg_abs(z):
    return jnp.log1p(jnp.exp(-jnp.abs(z)))


def _t5_bias(dist, rb_ref, h):
    val = jnp.full(dist.shape, rb_ref[REL_BUCKETS - 1, h], F32)
    for b in range(REL_BUCKETS - 2, -1, -1):
        val = jnp.where(dist < T5_STARTS[b + 1], rb_ref[b, h], val)
    return val


def _ada_kernel(c_ref, w_ref, b_ref, o_ref):
    c = c_ref[...]
    a = (c * jax.nn.sigmoid(c)).astype(BF16)
    o_ref[0] = jnp.dot(a, w_ref[0].astype(BF16), preferred_element_type=F32) + b_ref[0]


def ada_mod(c_all, w_ada, b_ada):
    nl, d, n = w_ada.shape
    mp = c_all.shape[0]
    tn = 1024
    return pl.pallas_call(
        _ada_kernel,
        grid=(nl, n // tn),
        in_specs=[pl.BlockSpec((mp, d), lambda l, j: (0, 0)),
                  pl.BlockSpec((1, d, tn), lambda l, j: (l, 0, j)),
                  pl.BlockSpec((1, 1, tn), lambda l, j: (l, 0, j))],
        out_specs=pl.BlockSpec((1, mp, tn), lambda l, j: (l, 0, j)),
        out_shape=jax.ShapeDtypeStruct((nl, mp, n), F32),
        compiler_params=_cparams(2, 40),
        name="ada_mod",
    )(c_all, w_ada, b_ada.reshape(nl, 1, n))


def _nmm_kernel(x_ref, g_ref, sc_ref, sh_ref, w_ref, o_ref, h_scr):
    @pl.when(pl.program_id(1) == 0)
    def _():
        x = x_ref[...]
        ms = jnp.mean(x * x, axis=-1, keepdims=True)
        y = x * lax.rsqrt(ms + EPS) * g_ref[...]
        h_scr[...] = (y * (1.0 + sc_ref[0]) + sh_ref[0]).astype(BF16)

    o_ref[...] = jnp.dot(h_scr[...], w_ref[...], preferred_element_type=F32)


def norm_mod_matmul(x, g, sc, sh, w, seq_rows, tm, tn):
    m, d = x.shape
    n = w.shape[1]
    r = sc.shape[1]
    if r == 1:
        mod_idx = lambda i, j: ((i * tm) // seq_rows, 0, 0)
    else:
        mod_idx = lambda i, j: (i, 0, 0)
    return pl.pallas_call(
        _nmm_kernel,
        grid=(m // tm, n // tn),
        in_specs=[pl.BlockSpec((tm, d), lambda i, j: (i, 0)),
                  pl.BlockSpec((1, d), lambda i, j: (0, 0)),
                  pl.BlockSpec((1, r, d), mod_idx),
                  pl.BlockSpec((1, r, d), mod_idx),
                  pl.BlockSpec((d, tn), lambda i, j: (0, j))],
        out_specs=pl.BlockSpec((tm, tn), lambda i, j: (i, j)),
        out_shape=jax.ShapeDtypeStruct((m, n), F32),
        scratch_shapes=[pltpu.VMEM((tm, d), BF16)],
        compiler_params=_cparams(2, 48),
        name="norm_mod_matmul",
    )(x, g.reshape(1, d), sc, sh, w)


def _mmres_kernel(*refs, n_a):
    a_refs = refs[:n_a]
    w_refs = refs[n_a:2 * n_a]
    xres_ref, gate_ref, o_ref = refs[2 * n_a:]
    acc = jnp.dot(a_refs[0][...].astype(BF16), w_refs[0][...], preferred_element_type=F32)
    for a_ref, w_ref in zip(a_refs[1:], w_refs[1:]):
        acc = acc + jnp.dot(a_ref[...].astype(BF16), w_ref[...], preferred_element_type=F32)
    o_ref[...] = xres_ref[...] + gate_ref[0] * acc


def matmul_gated_residual(a_list, w, xres, gate, seq_rows, tm, tn):
    m, n = xres.shape
    n_a = len(a_list)
    kp = a_list[0].shape[1]
    r = gate.shape[1]
    if r == 1:
        mod_idx = lambda i, j: ((i * tm) // seq_rows, 0, 0)
    else:
        mod_idx = lambda i, j: (i, 0, 0)
    in_specs = [pl.BlockSpec((tm, kp), lambda i, j: (i, 0)) for _ in range(n_a)]
    in_specs += [pl.BlockSpec((kp, tn), functools.partial(lambda i, j, p: (p, j), p=p)) for p in range(n_a)]
    in_specs += [pl.BlockSpec((tm, tn), lambda i, j: (i, j)),
                 pl.BlockSpec((1, r, tn), lambda i, j: mod_idx(i, j)[:2] + (j,))]
    return pl.pallas_call(
        functools.partial(_mmres_kernel, n_a=n_a),
        grid=(m // tm, n // tn),
        in_specs=in_specs,
        out_specs=pl.BlockSpec((tm, tn), lambda i, j: (i, j)),
        out_shape=jax.ShapeDtypeStruct((m, n), F32),
        compiler_params=_cparams(2, 48),
        name="matmul_gated_residual",
    )(*a_list, *([w] * n_a), xres, gate)


def _conv_head(xh, buf_ref, w, width):
    nb = buf_ref.shape[0]
    c = xh.shape[-1]
    t = lax.broadcasted_iota(jnp.int32, xh.shape, 0) % SUBLANES
    y = xh * w[width - 1:width]
    for s in range(1, width):
        sh = pltpu.roll(xh, s, 0)
        for tt in range(s):
            k = tt + width - 1 - s
            brow = buf_ref[:, k:k + 1, :]
            bsel = jnp.broadcast_to(brow, (nb, SUBLANES, c)).reshape(nb * SUBLANES, c)
            sh = jnp.where(t == tt, bsel, sh)
        y = y + sh * w[width - 1 - s:width - s]
    return y


def _conv_bulk(x2, w, width):
    y = x2 * w[width - 1:width]
    for s in range(1, width):
        y = y + pltpu.roll(x2, s, 0) * w[width - 1 - s:width - s]
    return y


def _conv_apply(x_ref, buf_ref, w_ref, t_len, width, emit):
    nb = x_ref.shape[0]
    c = x_ref.shape[2]
    w = w_ref[...]
    if t_len == SUBLANES:
        xh = x_ref[...].reshape(nb * SUBLANES, c)
        emit(None, _conv_head(xh, buf_ref, w, width))
    else:
        assert nb == 1
        emit(None, _conv_bulk(x_ref[0], w, width))
        emit(SUBLANES, _conv_head(x_ref[0, 0:SUBLANES, :], buf_ref, w, width))


def _dconv_kernel(x_ref, buf_ref, w_ref, o_ref, *, t_len):
    j = pl.program_id(1)
    nb = x_ref.shape[0]

    def emit(rows, y):
        y = y * jax.nn.sigmoid(y)
        nrm = y * lax.rsqrt(jnp.sum(y * y, axis=-1, keepdims=True) + EPS)
        y = jnp.where(j < H, nrm * (DH ** -0.5), jnp.where(j < 2 * H, nrm, y))
        if rows is None:
            o_ref[...] = y.reshape(o_ref.shape)
        else:
            o_ref[0, 0:rows, :] = y

    _conv_apply(x_ref, buf_ref, w_ref, t_len, SC_W, emit)


def delta_conv(proj3, buf, w, nb):
    b, t, _ = proj3.shape
    return pl.pallas_call(
        functools.partial(_dconv_kernel, t_len=t),
        grid=(b // nb, A_CONV // LANES),
        in_specs=[pl.BlockSpec((nb, t, LANES), lambda i, j: (i, 0, j)),
                  pl.BlockSpec((nb, SC_W - 1, LANES), lambda i, j: (i, 0, j)),
                  pl.BlockSpec((SC_W, LANES), lambda i, j: (0, j))],
        out_specs=pl.BlockSpec((nb, t, LANES), lambda i, j: (i, 0, j)),
        out_shape=jax.ShapeDtypeStruct((b, t, A_CONV), F32),
        compiler_params=_cparams(2, 32),
        name="delta_conv",
    )(proj3, buf, w)


def _ffn_act_kernel(xg_ref, xu_ref, bg_ref, bu_ref, wg_ref, wu_ref, cg_ref, cu_ref, o_ref, *, t_len):
    outs = {}

    def emit_g(rows, y):
        outs[("g", rows)] = y + cg_ref[...]

    def emit_u(rows, y):
        outs[("u", rows)] = y + cu_ref[...]

    _conv_apply(xg_ref, bg_ref, wg_ref, t_len, FFN_CONV_W, emit_g)
    _conv_apply(xu_ref, bu_ref, wu_ref, t_len, FFN_CONV_W, emit_u)
    for (kind, rows), gate in outs.items():
        if kind != "g":
            continue
        y = (gate * jax.nn.sigmoid(gate) * outs[("u", rows)]).astype(o_ref.dtype)
        if rows is None:
            o_ref[...] = y.reshape(o_ref.shape)
        else:
            o_ref[0, 0:rows, :] = y


def ffn_act(u3, buf, w, bias, nb, tc):
    b, t, f2 = u3.shape
    f = f2 // 2
    nj = f // tc
    bias2 = bias.reshape(1, f2)
    return pl.pallas_call(
        functools.partial(_ffn_act_kernel, t_len=t),
        grid=(b // nb, nj),
        in_specs=[pl.BlockSpec((nb, t, tc), lambda i, j: (i, 0, j)),
                  pl.BlockSpec((nb, t, tc), lambda i, j: (i, 0, j + nj)),
                  pl.BlockSpec((nb, FFN_CONV_W - 1, tc), lambda i, j: (i, 0, j)),
                  pl.BlockSpec((nb, FFN_CONV_W - 1, tc), lambda i, j: (i, 0, j + nj)),
                  pl.BlockSpec((FFN_CONV_W, tc), lambda i, j: (0, j)),
                  pl.BlockSpec((FFN_CONV_W, tc), lambda i, j: (0, j + nj)),
                  pl.BlockSpec((1, tc), lambda i, j: (0, j)),
                  pl.BlockSpec((1, tc), lambda i, j: (0, j + nj))],
        out_specs=pl.BlockSpec((nb, t, tc), lambda i, j: (i, 0, j)),
        out_shape=jax.ShapeDtypeStruct((b, t, f), F32),
        compiler_params=_cparams(2, 48),
        name="ffn_act",
    )(u3, u3, buf, buf, w, w, bias2, bias2)


def _hnorm_kernel(x_ref, g_ref, o_ref, *, group):
    x = x_ref[...]
    s = x * x
    if group == LANES:
        ms = jnp.mean(s, axis=-1, keepdims=True)
    else:
        lane = lax.broadcasted_iota(jnp.int32, x.shape, 1)
        lo = jnp.sum(jnp.where(lane < group, s, 0.0), axis=-1, keepdims=True)
        hi = jnp.sum(jnp.where(lane >= group, s, 0.0), axis=-1, keepdims=True)
        ms = jnp.where(lane < group, lo, hi) * (1.0 / group)
    o_ref[...] = x * lax.rsqrt(ms + EPS) * g_ref[...]


def head_norm(proj, col0, ncols, gain, tm):
    m = proj.shape[0]
    group = gain.shape[0]
    g = jnp.tile(gain, LANES // group).reshape(1, LANES)
    cb = col0 // LANES
    return pl.pallas_call(
        functools.partial(_hnorm_kernel, group=group),
        grid=(m // tm, ncols // LANES),
        in_specs=[pl.BlockSpec((tm, LANES), lambda i, j: (i, cb + j)),
                  pl.BlockSpec((1, LANES), lambda i, j: (0, 0))],
        out_specs=pl.BlockSpec((tm, LANES), lambda i, j: (i, j)),
        out_shape=jax.ShapeDtypeStruct((m, ncols), F32),
        compiler_params=_cparams(2, 32),
        name="head_norm",
    )(proj, g)


def _onorm_kernel(o_ref, z_ref, g_ref, y_ref):
    o = o_ref[...]
    z = z_ref[...]
    y = o * lax.rsqrt(jnp.mean(o * o, axis=-1, keepdims=True) + EPS) * g_ref[...]
    y_ref[...] = y * (z * jax.nn.sigmoid(z))


def onorm_gate(o, proj, gain, tm):
    m = o.shape[0]
    zb = Z_A // LANES
    return pl.pallas_call(
        _onorm_kernel,
        grid=(m // tm, H),
        in_specs=[pl.BlockSpec((tm, LANES), lambda i, j: (i, j)),
                  pl.BlockSpec((tm, LANES), lambda i, j: (i, zb + j)),
                  pl.BlockSpec((1, LANES), lambda i, j: (0, 0))],
        out_specs=pl.BlockSpec((tm, LANES), lambda i, j: (i, j)),
        out_shape=jax.ShapeDtypeStruct((m, H * DH), F32),
        compiler_params=_cparams(2, 32),
        name="onorm_gate",
    )(o, proj, gain.reshape(1, LANES))


def _gates_kernel(s_ref, bias_ref, alog_ref, g_ref, c_ref, carry, *, tm):
    @pl.when(pl.program_id(1) == 0)
    def _():
        carry[...] = jnp.zeros_like(carry)

    t = s_ref[0] + bias_ref[...]
    lane = lax.broadcasted_iota(jnp.int32, t.shape, 1)
    row = lax.broadcasted_iota(jnp.int32, t.shape, 0)
    l1p = _softplus_neg_abs(t)
    sig = jax.nn.sigmoid(t)
    g = -jnp.exp(alog_ref[...]) * (jnp.maximum(t, 0.0) + l1p)
    lsig = jnp.minimum(t, 0.0) - l1p
    gt = jnp.where(lane < G_L, sig, jnp.where(lane < LOGF_L, g, lsig))
    gt = jnp.where(lane < LOGF_L + H, gt, 0.0)
    g_ref[0] = gt
    cs = gt
    k = 1
    while k < tm:
        cs = cs + jnp.where(row >= k, pltpu.roll(cs, k, 0), 0.0)
        k *= 2
    cs = cs + carry[0:1, :]
    c_ref[0] = cs
    carry[...] = jnp.broadcast_to(cs[tm - 1:tm, :], carry.shape)


def even_gates(proj3, bias_vec, alog_vec, tm):
    b, t, _ = proj3.shape
    sb = SMALL // LANES
    return pl.pallas_call(
        functools.partial(_gates_kernel, tm=tm),
        grid=(b, t // tm),
        in_specs=[pl.BlockSpec((1, tm, LANES), lambda i, j: (i, j, sb)),
                  pl.BlockSpec((1, LANES), lambda i, j: (0, 0)),
                  pl.BlockSpec((1, LANES), lambda i, j: (0, 0))],
        out_specs=[pl.BlockSpec((1, tm, LANES), lambda i, j: (i, j, 0)),
                   pl.BlockSpec((1, tm, LANES), lambda i, j: (i, j, 0))],
        out_shape=[jax.ShapeDtypeStruct((b, t, LANES), F32),
                   jax.ShapeDtypeStruct((b, t, LANES), F32)],
        scratch_shapes=[pltpu.VMEM((SUBLANES, LANES), F32)],
        compiler_params=_cparams(2, 32),
        name="even_gates",
    )(proj3, bias_vec, alog_vec)


def _bmm(a, b, spec):
    return jnp.einsum(spec, a, b, precision=HI, preferred_element_type=F32)


def _delta_kernel(q_ref, k_ref, v_ref, gt_ref, s0_ref, o_ref, sn_ref,
                  wk_s, u0_s, qg_s, kd_s, qk_s, eg_s, *, t_len):
    L = DELTA_CHUNK
    nc = t_len // L
    h = pl.program_id(1)
    gt = gt_ref[0]
    lane = lax.broadcasted_iota(jnp.int32, gt.shape, 1)
    pos = lax.broadcasted_iota(jnp.int32, gt.shape, 0) % L
    onehot = (lane == G_L + h).astype(F32)
    cs = gt * onehot
    step = 1
    while step < L:
        cs = cs + jnp.where(pos >= step, pltpu.roll(cs, step, 0), 0.0)
        step *= 2
    q3 = q_ref[0].reshape(nc, L, DH)
    k3 = k_ref[0].reshape(nc, L, DH)
    v3 = v_ref[0].reshape(nc, L, DH)
    gt3 = gt.reshape(nc, L, LANES)
    cs3 = cs.reshape(nc, L, LANES)
    e3 = onehot.reshape(nc, L, LANES)
    lane3 = lax.broadcasted_iota(jnp.int32, gt3.shape, 2)
    beta = jnp.sum(jnp.where(lane3 == BETA_L + h, gt3, 0.0), axis=-1, keepdims=True)
    g3 = jnp.sum(cs3, axis=-1, keepdims=True)
    diff = _bmm(cs3, e3, 'cil,cjl->cij') - _bmm(e3, cs3, 'cil,cjl->cij')
    ii = lax.broadcasted_iota(jnp.int32, (nc, L, L), 1)
    jj = lax.broadcasted_iota(jnp.int32, (nc, L, L), 2)
    decay = jnp.where(ii >= jj, jnp.exp(jnp.where(ii >= jj, diff, 0.0)), 0.0)
    kb3 = k3 * beta
    a = jnp.where(ii > jj, _bmm(kb3, k3, 'cid,cjd->cij') * decay, 0.0)
    pk = -a
    tinv = (ii == jj).astype(F32) + pk
    span = 2
    while span < L:
        pk = _bmm(pk, pk, 'cij,cjk->cik')
        tinv = tinv + _bmm(tinv, pk, 'cij,cjk->cik')
        span *= 2
    eg = jnp.exp(g3)
    u0_s[...] = _bmm(tinv, v3 * beta, 'cij,cjd->cid')
    wk_s[...] = _bmm(tinv, kb3 * eg, 'cij,cjd->cid')
    qk_s[...] = _bmm(q3, k3, 'cid,cjd->cij') * decay
    glast = g3[:, L - 1:L, :]
    kd_s[...] = k3 * jnp.exp(glast - g3)
    qg_s[...] = q3 * eg
    eg_s[...] = jnp.broadcast_to(jnp.exp(glast), (nc, 1, LANES))

    def body(c, s):
        v_new = u0_s[c] - jnp.dot(wk_s[c], s, precision=HI, preferred_element_type=F32)
        o = (jnp.dot(qg_s[c], s, precision=HI, preferred_element_type=F32)
             + jnp.dot(qk_s[c], v_new, precision=HI, preferred_element_type=F32))
        o_ref[0, pl.ds(pl.multiple_of(c * L, L), L), :] = o
        return s * eg_s[c] + lax.dot_general(kd_s[c], v_new, (((0,), (0,)), ((), ())),
                                             precision=HI, preferred_element_type=F32)

    sn_ref[0, 0] = lax.fori_loop(0, nc, body, s0_ref[0, 0])


def delta_rule(qkv3, gates, s0):
    b, t, _ = qkv3.shape
    nc = t // DELTA_CHUNK
    L = DELTA_CHUNK
    return pl.pallas_call(
        functools.partial(_delta_kernel, t_len=t),
        grid=(b, H),
        in_specs=[pl.BlockSpec((1, t, DH), lambda i, j: (i, 0, j)),
                  pl.BlockSpec((1, t, DH), lambda i, j: (i, 0, H + j)),
                  pl.BlockSpec((1, t, DH), lambda i, j: (i, 0, 2 * H + j)),
                  pl.BlockSpec((1, t, LANES), lambda i, j: (i, 0, 0)),
                  pl.BlockSpec((1, 1, DH, DH), lambda i, j: (i, j, 0, 0))],
        out_specs=[pl.BlockSpec((1, t, DH), lambda i, j: (i, 0, j)),
                   pl.BlockSpec((1, 1, DH, DH), lambda i, j: (i, j, 0, 0))],
        out_shape=[jax.ShapeDtypeStruct((b, t, H * DH), F32),
                   jax.ShapeDtypeStruct((b, H, DH, DH), F32)],
        scratch_shapes=[pltpu.VMEM((nc, L, DH), F32), pltpu.VMEM((nc, L, DH), F32),
                        pltpu.VMEM((nc, L, DH), F32), pltpu.VMEM((nc, L, DH), F32),
                        pltpu.VMEM((nc, L, L), F32), pltpu.VMEM((nc, 1, LANES), F32)],
        compiler_params=_cparams(2, 48),
        name="delta_rule",
    )(qkv3, qkv3, qkv3, gates, s0)


def _last_kblock(qi, tq, tk):
    return ((qi + 1) * tq - 1) // tk


def _online_softmax(s, v_bf, m_ref, l_ref, acc_ref):
    m_prev = m_ref[...]
    m_new = jnp.maximum(m_prev, jnp.max(s, axis=-1, keepdims=True))
    alpha = jnp.exp(m_prev - m_new)
    p = jnp.exp(s - m_new)
    l_ref[...] = alpha * l_ref[...] + jnp.sum(p, axis=-1, keepdims=True)
    acc_ref[...] = alpha * acc_ref[...] + jnp.dot(p.astype(BF16), v_bf, preferred_element_type=F32)
    m_ref[...] = m_new


def _fox_p_kernel(q_ref, k_ref, v_ref, cq_ref, ck_ref, o_ref, m_s, l_s, acc_s, *, tq, tk):
    h = pl.program_id(1)
    qi = pl.program_id(2)
    ki = pl.program_id(3)
    last = _last_kblock(qi, tq, tk)

    @pl.when(ki == 0)
    def _():
        m_s[...] = jnp.full(m_s.shape, -jnp.inf, F32)
        l_s[...] = jnp.zeros_like(l_s)
        acc_s[...] = jnp.zeros_like(acc_s)

    @pl.when(ki <= last)
    def _():
        s = _nt(q_ref[0].astype(BF16), k_ref[0].astype(BF16)) * (DH ** -0.5)
        cqb = cq_ref[0]
        lane = lax.broadcasted_iota(jnp.int32, cqb.shape, 1)
        cq = jnp.sum(jnp.where(lane == LOGF_L + h, cqb, 0.0), axis=-1, keepdims=True)
        s = s + cq - ck_ref[0, 0]
        qpos = qi * tq + lax.broadcasted_iota(jnp.int32, s.shape, 0)
        kpos = ki * tk + lax.broadcasted_iota(jnp.int32, s.shape, 1)
        s = jnp.where(kpos <= qpos, s, -jnp.inf)
        _online_softmax(s, v_ref[0].astype(BF16), m_s, l_s, acc_s)

    @pl.when(ki == last)
    def _():
        o_ref[0] = acc_s[...] / l_s[...]


def fox_prompt(qn3, kn3, proj3, cum, cum_t, tq, tk):
    b, t, _ = qn3.shape
    vb = V_B // LANES
    kidx = lambda bi, h, qi, ki: jnp.minimum(ki, _last_kblock(qi, tq, tk))
    return pl.pallas_call(
        functools.partial(_fox_p_kernel, tq=tq, tk=tk),
        grid=(b, H, t // tq, t // tk),
        in_specs=[pl.BlockSpec((1, tq, DH), lambda bi, h, qi, ki: (bi, qi, h)),
                  pl.BlockSpec((1, tk, DH), lambda bi, h, qi, ki: (bi, kidx(bi, h, qi, ki), h)),
                  pl.BlockSpec((1, tk, DH), lambda bi, h, qi, ki: (bi, kidx(bi, h, qi, ki), vb + h)),
                  pl.BlockSpec((1, tq, LANES), lambda bi, h, qi, ki: (bi, qi, 0)),
                  pl.BlockSpec((1, 1, 1, tk), lambda bi, h, qi, ki: (bi, h, 0, kidx(bi, h, qi, ki)))],
        out_specs=pl.BlockSpec((1, tq, DH), lambda bi, h, qi, ki: (bi, qi, h)),
        out_shape=jax.ShapeDtypeStruct((b, t, H * DH), F32),
        scratch_shapes=[pltpu.VMEM((tq, 1), F32), pltpu.VMEM((tq, 1), F32), pltpu.VMEM((tq, DH), F32)],
        compiler_params=_cparams(4, 32),
        name="fox_prompt",
    )(qn3, kn3, proj3, cum, cum_t)


def _t5_tile_kernel(rb_ref, o_ref, *, tb):
    h = pl.program_id(0)
    off = pl.program_id(1)
    dist = (off * tb + lax.broadcasted_iota(jnp.int32, (tb, tb), 0)
            - lax.broadcasted_iota(jnp.int32, (tb, tb), 1))
    o_ref[0, 0] = _t5_bias(dist, rb_ref, h)


def t5_tiles(rel_bias, tb):
    return pl.pallas_call(
        functools.partial(_t5_tile_kernel, tb=tb),
        grid=(H, 2),
        in_specs=[pl.BlockSpec(memory_space=pltpu.SMEM)],
        out_specs=pl.BlockSpec((1, 1, tb, tb), lambda h, o: (h, o, 0, 0)),
        out_shape=jax.ShapeDtypeStruct((H, 2, tb, tb), F32),
        compiler_params=_cparams(2, 32),
        name="t5_tiles",
    )(rel_bias)


def _diff_lambda(lam_ref, lam_init):
    lp = lam_ref[...]
    s01 = jnp.sum(lp[0:1] * lp[1:2], axis=-1, keepdims=True)
    s23 = jnp.sum(lp[2:3] * lp[3:4], axis=-1, keepdims=True)
    return jnp.exp(s01) - jnp.exp(s23) + lam_init


def _diff_finish(acc0, l0, acc1, l1, lam, g, lam_init):
    o = acc0 / l0 - lam * (acc1 / l1)
    return o * lax.rsqrt(jnp.mean(o * o, axis=-1, keepdims=True) + EPS) * g * (1.0 - lam_init)


def _diff_p_kernel(rb_ref, q_ref, k_ref, v_ref, bias_ref, lam_ref, g_ref, o_ref,
                   m0, l0, a0, m1, l1, a1, *, tb, lam_init):
    h = pl.program_id(1)
    qi = pl.program_id(2)
    ki = pl.program_id(3)
    state = ((m0, l0, a0), (m1, l1, a1))

    @pl.when(ki == 0)
    def _():
        for m_s, l_s, a_s in state:
            m_s[...] = jnp.full(m_s.shape, -jnp.inf, F32)
            l_s[...] = jnp.zeros_like(l_s)
            a_s[...] = jnp.zeros_like(a_s)

    @pl.when(ki <= qi)
    def _():
        qf = q_ref[0]
        k = k_ref[0].astype(BF16)
        v = v_ref[0].astype(BF16)
        lane = lax.broadcasted_iota(jnp.int32, qf.shape, 1)
        near = (qi - ki) <= 1
        bias = jnp.where(near, bias_ref[0, 0], rb_ref[REL_BUCKETS - 1, h])
        qpos = qi * tb + lax.broadcasted_iota(jnp.int32, (tb, tb), 0)
        kpos = ki * tb + lax.broadcasted_iota(jnp.int32, (tb, tb), 1)
        for mi, (m_s, l_s, a_s) in enumerate(state):
            half = (lane < DQ_C) if mi == 0 else (lane >= DQ_C)
            qm = jnp.where(half, qf, 0.0).astype(BF16)
            s = _nt(qm, k) * (DQ_C ** -0.5) + bias
            s = jnp.where(kpos <= qpos, s, -jnp.inf)
            _online_softmax(s, v, m_s, l_s, a_s)

    @pl.when(ki == qi)
    def _():
        lam = _diff_lambda(lam_ref, lam_init)
        o_ref[0] = _diff_finish(a0[...], l0[...], a1[...], l1[...], lam, g_ref[...], lam_init)


def diff_prompt(qn3, kn3, proj3, v_col0, tiles, rel_bias, lam_p, subln_g, lam_init, tb):
    b, t, _ = qn3.shape
    vb = v_col0 // LANES
    kidx = lambda qi, ki: jnp.minimum(ki, qi)
    return pl.pallas_call(
        functools.partial(_diff_p_kernel, tb=tb, lam_init=lam_init),
        grid=(b, H, t // tb, t // tb),
        in_specs=[pl.BlockSpec(memory_space=pltpu.SMEM),
                  pl.BlockSpec((1, tb, DH), lambda bi, h, qi, ki: (bi, qi, h)),
                  pl.BlockSpec((1, tb, DH), lambda bi, h, qi, ki: (bi, kidx(qi, ki), h)),
                  pl.BlockSpec((1, tb, DH), lambda bi, h, qi, ki: (bi, kidx(qi, ki), vb + h)),
                  pl.BlockSpec((1, 1, tb, tb), lambda bi, h, qi, ki: (h, jnp.clip(qi - ki, 0, 1), 0, 0)),
                  pl.BlockSpec((4, DQ_C), lambda bi, h, qi, ki: (0, 0)),
                  pl.BlockSpec((1, DH), lambda bi, h, qi, ki: (0, 0))],
        out_specs=pl.BlockSpec((1, tb, DH), lambda bi, h, qi, ki: (bi, qi, h)),
        out_shape=jax.ShapeDtypeStruct((b, t, H * DH), F32),
        scratch_shapes=[pltpu.VMEM((tb, 1), F32), pltpu.VMEM((tb, 1), F32), pltpu.VMEM((tb, DH), F32),
                        pltpu.VMEM((tb, 1), F32), pltpu.VMEM((tb, 1), F32), pltpu.VMEM((tb, DH), F32)],
        compiler_params=_cparams(4, 32),
        name="diff_prompt",
    )(rel_bias, qn3, kn3, proj3, tiles, lam_p, subln_g.reshape(1, DH))


def _sb_weights(z, mask, carry, u):
    l1p = _softplus_neg_abs(z)
    log_sig = jnp.minimum(z, 0.0) - l1p
    log_keep = jnp.minimum(-z, 0.0) - l1p
    if mask is not None:
        log_keep = jnp.where(mask, log_keep, 0.0)
    later = carry + jnp.dot(log_keep, u, precision=HI, preferred_element_type=F32)
    a = jnp.exp(log_sig + later)
    if mask is not None:
        a = jnp.where(mask, a, 0.0)
    return a, jnp.sum(log_keep, axis=-1, keepdims=True)


def _suffix_matrix(n):
    return (lax.broadcasted_iota(jnp.int32, (n, n), 0) > lax.broadcasted_iota(jnp.int32, (n, n), 1)).astype(F32)


def _sb_p_kernel(q_ref, k_ref, v_ref, o_ref, carry_s, acc_s, *, tq, tk):
    qi = pl.program_id(2)
    ki = pl.program_id(3)
    last = _last_kblock(qi, tq, tk)

    @pl.when(ki == 0)
    def _():
        carry_s[...] = jnp.zeros_like(carry_s)
        acc_s[...] = jnp.zeros_like(acc_s)

    @pl.when(ki <= last)
    def _():
        kb = last - ki
        z = _nt(q_ref[0].astype(BF16), k_ref[0].astype(BF16)) * (DH ** -0.5)
        qpos = qi * tq + lax.broadcasted_iota(jnp.int32, z.shape, 0)
        kpos = kb * tk + lax.broadcasted_iota(jnp.int32, z.shape, 1)
        a, rs = _sb_weights(z, kpos < qpos, carry_s[...], _suffix_matrix(tk))
        acc_s[...] += jnp.dot(a.astype(BF16), v_ref[0].astype(BF16), preferred_element_type=F32)
        carry_s[...] += rs

    @pl.when(ki == last)
    def _():
        o_ref[0] = acc_s[...]


def sb_prompt(proj3, q_col0, k_col0, v_col0, tq, tk):
    b, t, _ = proj3.shape
    qb, kb, vb = q_col0 // LANES, k_col0 // LANES, v_col0 // LANES
    kidx = lambda qi, ki: jnp.maximum(_last_kblock(qi, tq, tk) - ki, 0)
    return pl.pallas_call(
        functools.partial(_sb_p_kernel, tq=tq, tk=tk),
        grid=(b, H, t // tq, t // tk),
        in_specs=[pl.BlockSpec((1, tq, DH), lambda bi, h, qi, ki: (bi, qi, qb + h)),
                  pl.BlockSpec((1, tk, DH), lambda bi, h, qi, ki: (bi, kidx(qi, ki), kb + h)),
                  pl.BlockSpec((1, tk, DH), lambda bi, h, qi, ki: (bi, kidx(qi, ki), vb + h))],
        out_specs=pl.BlockSpec((1, tq, DH), lambda bi, h, qi, ki: (bi, qi, h)),
        out_shape=jax.ShapeDtypeStruct((b, t, H * DH), F32),
        scratch_shapes=[pltpu.VMEM((tq, 1), F32), pltpu.VMEM((tq, DH), F32)],
        compiler_params=_cparams(4, 32),
        name="sb_prompt",
    )(proj3, proj3, proj3)


def _page_of(pt_ref, b, p, n_pages, layer_off):
    return layer_off + pt_ref[b, n_pages - 1 - jnp.maximum(p - 1, 0)]


def _head_cols(ref, h):
    return ref[0, :, h * DH:(h + 1) * DH]


def _fox_s_kernel(pt_ref, q_ref, kn_ref, vn_ref, lfn_ref, cumn_ref, kp_ref, vp_ref, lfp_ref, o_ref,
                  m_s, l_s, acc_s, carry_s, *, n_pages, t_len):
    p = pl.program_id(1)
    tio = lax.broadcasted_iota(jnp.int32, (t_len, PAGE), 0)
    jio = lax.broadcasted_iota(jnp.int32, (t_len, PAGE), 1)
    u = _suffix_matrix(PAGE)

    def process(k_ref, v_ref, lf_ref, is_new):
        lf_t = lf_ref[0]
        suf = jnp.dot(lf_t, u, precision=HI, preferred_element_type=F32)
        tot = jnp.sum(lf_t, axis=-1, keepdims=True)
        cumn = cumn_ref[0]
        for h in range(H):
            s = _nt(_head_cols(q_ref, h).astype(BF16), _head_cols(k_ref, h).astype(BF16)) * (DH ** -0.5)
            base = cumn[:, LOGF_L + h:LOGF_L + h + 1]
            if is_new:
                rowoff = base - tot[h:h + 1, :]
            else:
                rowoff = base + carry_s[h:h + 1, :]
            s = s + rowoff + suf[h:h + 1, :]
            if is_new:
                s = jnp.where(jio <= tio, s, -jnp.inf)
            _online_softmax(s, _head_cols(v_ref, h).astype(BF16), m_s.at[h], l_s.at[h],
                            acc_s.at[:, h * DH:(h + 1) * DH])
        if not is_new:
            carry_s[...] += tot

    @pl.when(p == 0)
    def _():
        m_s[...] = jnp.full(m_s.shape, -jnp.inf, F32)
        l_s[...] = jnp.zeros_like(l_s)
        acc_s[...] = jnp.zeros_like(acc_s)
        carry_s[...] = jnp.zeros_like(carry_s)
        process(kn_ref, vn_ref, lfn_ref, True)

    @pl.when(p > 0)
    def _():
        process(kp_ref, vp_ref, lfp_ref, False)

    @pl.when(p == n_pages)
    def _():
        for h in range(H):
            o_ref[0, :, h * DH:(h + 1) * DH] = acc_s[:, h * DH:(h + 1) * DH] / l_s[h]


def _paged_specs(t_len, n_pages, layer_off, n_q=1, n_new=2, n_pool=2, extra_new=(), extra_pool=()):
    row = H * DH
    specs = [pl.BlockSpec((1, t_len, row), lambda b, p, pt: (b, 0, 0)) for _ in range(n_q)]
    specs += [pl.BlockSpec((1, PAGE, row), lambda b, p, pt: (b, 0, 0)) for _ in range(n_new)]
    specs += list(extra_new)
    specs += [pl.BlockSpec((1, PAGE, row), lambda b, p, pt: (_page_of(pt, b, p, n_pages, layer_off), 0, 0))
              for _ in range(n_pool)]
    specs += list(extra_pool)
    return specs


def fox_sample(pt, qn3, kn_pad, vn_pad, lfn_t, cumn, kpool, vpool, lfpool_t, layer_off):
    b, t, row = qn3.shape
    n_pages = pt.shape[1]
    extra_new = [pl.BlockSpec((1, H, PAGE), lambda bi, p, ptr: (bi, 0, 0)),
                 pl.BlockSpec((1, t, LANES), lambda bi, p, ptr: (bi, 0, 0))]
    extra_pool = [pl.BlockSpec((1, H, PAGE), lambda bi, p, ptr: (_page_of(ptr, bi, p, n_pages, layer_off), 0, 0))]
    return pl.pallas_call(
        functools.partial(_fox_s_kernel, n_pages=n_pages, t_len=t),
        grid_spec=pltpu.PrefetchScalarGridSpec(
            num_scalar_prefetch=1,
            grid=(b, n_pages + 1),
            in_specs=_paged_specs(t, n_pages, layer_off, extra_new=extra_new, extra_pool=extra_pool),
            out_specs=pl.BlockSpec((1, t, row), lambda bi, p, ptr: (bi, 0, 0)),
            scratch_shapes=[pltpu.VMEM((H, t, 1), F32), pltpu.VMEM((H, t, 1), F32),
                            pltpu.VMEM((t, row), F32), pltpu.VMEM((H, 1), F32)]),
        out_shape=jax.ShapeDtypeStruct((b, t, row), F32),
        compiler_params=_cparams(2, 32),
        name="fox_sample",
    )(pt, qn3, kn_pad, vn_pad, lfn_t, cumn, kpool, vpool, lfpool_t)


def _diff_s_kernel(pt_ref, rb_ref, q_ref, kn_ref, vn_ref, kp_ref, vp_ref, lam_ref, g_ref, o_ref,
                   m_s, l_s, acc_s, *, n_pages, t_len, lam_init):
    p = pl.program_id(1)
    tio = lax.broadcasted_iota(jnp.int32, (t_len, PAGE), 0)
    jio = lax.broadcasted_iota(jnp.int32, (t_len, PAGE), 1)
    lane = lax.broadcasted_iota(jnp.int32, (t_len, DH), 1)
    past = n_pages * PAGE

    def process(k_ref, v_ref, kpos0, is_new):
        dist = (past + tio) - (kpos0 + jio)
        for h in range(H):
            bias = _t5_bias(dist, rb_ref, h)
            qf = _head_cols(q_ref, h)
            k = _head_cols(k_ref, h).astype(BF16)
            v = _head_cols(v_ref, h).astype(BF16)
            for mi in range(2):
                half = (lane < DQ_C) if mi == 0 else (lane >= DQ_C)
                s = _nt(jnp.where(half, qf, 0.0).astype(BF16), k) * (DQ_C ** -0.5) + bias
                if is_new:
                    s = jnp.where(dist >= 0, s, -jnp.inf)
                idx = 2 * h + mi
                _online_softmax(s, v, m_s.at[idx], l_s.at[idx], acc_s.at[mi, :, h * DH:(h + 1) * DH])

    @pl.when(p == 0)
    def _():
        m_s[...] = jnp.full(m_s.shape, -jnp.inf, F32)
        l_s[...] = jnp.zeros_like(l_s)
        acc_s[...] = jnp.zeros_like(acc_s)
        process(kn_ref, vn_ref, past, True)

    @pl.when(p > 0)
    def _():
        process(kp_ref, vp_ref, (n_pages - p) * PAGE, False)

    @pl.when(p == n_pages)
    def _():
        lam = _diff_lambda(lam_ref, lam_init)
        for h in range(H):
            cols = slice(h * DH, (h + 1) * DH)
            o_ref[0, :, cols] = _diff_finish(acc_s[0, :, cols], l_s[2 * h], acc_s[1, :, cols], l_s[2 * h + 1],
                                             lam, g_ref[...], lam_init)


def diff_sample(pt, rel_bias, qn3, kn_pad, vn_pad, kpool, vpool, lam_p, subln_g, lam_init, layer_off):
    b, t, row = qn3.shape
    n_pages = pt.shape[1]
    in_specs = [pl.BlockSpec(memory_space=pltpu.SMEM)]
    in_specs += _paged_specs(t, n_pages, layer_off)
    in_specs += [pl.BlockSpec((4, DQ_C), lambda bi, p, ptr: (0, 0)),
                 pl.BlockSpec((1, DH), lambda bi, p, ptr: (0, 0))]
    return pl.pallas_call(
        functools.partial(_diff_s_kernel, n_pages=n_pages, t_len=t, lam_init=lam_init),
        grid_spec=pltpu.PrefetchScalarGridSpec(
            num_scalar_prefetch=1,
            grid=(b, n_pages + 1),
            in_specs=in_specs,
            out_specs=pl.BlockSpec((1, t, row), lambda bi, p, ptr: (bi, 0, 0)),
            scratch_shapes=[pltpu.VMEM((2 * H, t, 1), F32), pltpu.VMEM((2 * H, t, 1), F32),
                            pltpu.VMEM((2, t, row), F32)]),
        out_shape=jax.ShapeDtypeStruct((b, t, row), F32),
        compiler_params=_cparams(2, 32),
        name="diff_sample",
    )(pt, rel_bias, qn3, kn_pad, vn_pad, kpool, vpool, lam_p, subln_g.reshape(1, DH))


def _sb_s_kernel(pt_ref, q_ref, kn_ref, vn_ref, kp_ref, vp_ref, o_ref, carry_s, acc_s, *, n_pages, t_len):
    p = pl.program_id(1)
    tio = lax.broadcasted_iota(jnp.int32, (t_len, PAGE), 0)
    jio = lax.broadcasted_iota(jnp.int32, (t_len, PAGE), 1)
    u = _suffix_matrix(PAGE)

    def process(k_ref, v_ref, is_new):
        mask = (jio < tio) if is_new else None
        for h in range(H):
            z = _nt(_head_cols(q_ref, h).astype(BF16), _head_cols(k_ref, h).astype(BF16)) * (DH ** -0.5)
            a, rs = _sb_weights(z, mask, carry_s[h], u)
            acc_s[:, h * DH:(h + 1) * DH] += jnp.dot(a.astype(BF16), _head_cols(v_ref, h).astype(BF16),
                                                     preferred_element_type=F32)
            carry_s[h] += rs

    @pl.when(p == 0)
    def _():
        carry_s[...] = jnp.zeros_like(carry_s)
        acc_s[...] = jnp.zeros_like(acc_s)
        process(kn_ref, vn_ref, True)

    @pl.when(p > 0)
    def _():
        process(kp_ref, vp_ref, False)

    @pl.when(p == n_pages)
    def _():
        o_ref[0] = acc_s[...]


def sb_sample(pt, q3, kn_pad, vn_pad, kpool, vpool, layer_off):
    b, t, row = q3.shape
    n_pages = pt.shape[1]
    return pl.pallas_call(
        functools.partial(_sb_s_kernel, n_pages=n_pages, t_len=t),
        grid_spec=pltpu.PrefetchScalarGridSpec(
            num_scalar_prefetch=1,
            grid=(b, n_pages + 1),
            in_specs=_paged_specs(t, n_pages, layer_off),
            out_specs=pl.BlockSpec((1, t, row), lambda bi, p, ptr: (bi, 0, 0)),
            scratch_shapes=[pltpu.VMEM((H, t, 1), F32), pltpu.VMEM((t, row), F32)]),
        out_shape=jax.ShapeDtypeStruct((b, t, row), F32),
        compiler_params=_cparams(2, 32),
        name="sb_sample",
    )(pt, q3, kn_pad, vn_pad, kpool, vpool)


def _pad_rows(a, rows):
    return jnp.pad(a, ((0, 0), (0, rows - a.shape[1]), (0, 0)))


def _mod_arrays(mod, idx, t, per_token):
    m = mod[:, idx]
    if per_token:
        return jnp.repeat(m, t, axis=0)[None]
    return m[:, None, :]


ROW_TILE = 512
COL_TILE = 512
GATE_TILE = 512
FOX_TILES = (512, 512)
DIFF_TILE = 256
SB_TILES = (256, 256)


def _tiles(b, t, per_token):
    return min(ROW_TILE, b * t if per_token else t), COL_TILE


def _even_layer(x2, b, t, mod, wts, i, past):
    per_token = past is not None
    tm, tn = _tiles(b, t, per_token)
    proj = norm_mod_matmul(x2, wts['norm_g'][2 * i, 0], _mod_arrays(mod, 1, t, per_token),
                           _mod_arrays(mod, 0, t, per_token), wts['w_in_even'][i], t, tm, tn)
    proj3 = proj.reshape(b, t, EVEN_N)
    if past is None:
        conv_buf = jnp.zeros((b, SC_W - 1, A_CONV), F32)
        s0 = jnp.zeros((b, H, DH, DH), F32)
    else:
        conv_buf = past['state_conv_delta'][i]
        s0 = past['state_delta'][i]
    qkv3 = delta_conv(proj3, conv_buf, wts['delta_conv_w'][i], 1 if t > SUBLANES else b)
    gates, cum = even_gates(proj3, wts['gate_bias'][i], wts['gate_alog'][i], min(t, GATE_TILE))
    t_pad = -(-t // DELTA_CHUNK) * DELTA_CHUNK
    o_a, s_new = delta_rule(_pad_rows(qkv3, t_pad), _pad_rows(gates, t_pad), s0)
    o_a = onorm_gate(o_a[:, :t].reshape(b * t, H * DH), proj, wts['delta_onorm_g'][i], tm)
    qn = head_norm(proj, Q_B, H * DH, wts['fox_qn_g'][i], tm)
    kn = head_norm(proj, K_B, H * DH, wts['fox_kn_g'][i], tm)
    qn3 = qn.reshape(b, t, H * DH)
    kn3 = kn.reshape(b, t, H * DH)
    v3 = proj3[:, :, V_B:V_B + H * DH]
    if past is None:
        cum_t = jnp.swapaxes(cum[:, :, LOGF_L:LOGF_L + H], 1, 2).reshape(b, H, 1, t)
        o_b = fox_prompt(qn3, kn3, proj3, cum, cum_t, min(t, FOX_TILES[0]), min(t, FOX_TILES[1]))
    else:
        pt = past['page_table']
        n_pool = past['cache_k_fox'].shape[1]
        lfn_t = _pad_rows(jnp.swapaxes(gates[:, :, LOGF_L:LOGF_L + H], 1, 2), H)
        lfn_t = jnp.pad(lfn_t, ((0, 0), (0, 0), (0, PAGE - t)))
        o_b = fox_sample(pt, qn3, _pad_rows(kn3, PAGE), _pad_rows(v3, PAGE), lfn_t, cum,
                         past['cache_k_fox'].reshape(-1, PAGE, H * DH),
                         past['cache_v_fox'].reshape(-1, PAGE, H * DH),
                         past['cache_logf_fox_t'], i * n_pool)
    m_out = (o_a, o_b.reshape(b * t, H * DH))
    rows = (kn3.reshape(b, t, H, DH), v3.reshape(b, t, H, DH), gates[:, :, LOGF_L:LOGF_L + H], s_new,
            proj3[:, t - (SC_W - 1):, :A_CONV])
    return m_out, rows


def _odd_layer(x2, b, t, mod, wts, i, l, past):
    per_token = past is not None
    tm, tn = _tiles(b, t, per_token)
    row = H * DH
    proj = norm_mod_matmul(x2, wts['norm_g'][l, 0], _mod_arrays(mod, 1, t, per_token),
                           _mod_arrays(mod, 0, t, per_token), wts['w_in_odd'][i], t, tm, tn)
    proj3 = proj.reshape(b, t, 6 * row)
    lam_init = 0.8 - 0.6 * math.exp(-0.3 * l)
    qcn3 = head_norm(proj, 0, row, wts['diff_qn_g'][i], tm).reshape(b, t, row)
    kcn3 = head_norm(proj, row, row, wts['diff_kn_g'][i], tm).reshape(b, t, row)
    vc3 = proj3[:, :, 2 * row:3 * row]
    kd3 = proj3[:, :, 4 * row:5 * row]
    vd3 = proj3[:, :, 5 * row:6 * row]
    if past is None:
        tb = min(t, DIFF_TILE)
        tiles = t5_tiles(wts['rel_bias'], tb)
        o_c = diff_prompt(qcn3, kcn3, proj3, 2 * row, tiles, wts['rel_bias'], wts['diff_lambda'][i],
                          wts['diff_subln_g'][i], lam_init, tb)
        o_d = sb_prompt(proj3, 3 * row, 4 * row, 5 * row, min(t, SB_TILES[0]), min(t, SB_TILES[1]))
    else:
        pt = past['page_table']
        n_pool = past['cache_k_diff'].shape[1]
        o_c = diff_sample(pt, wts['rel_bias'], qcn3, _pad_rows(kcn3, PAGE), _pad_rows(vc3, PAGE),
                          past['cache_k_diff'].reshape(-1, PAGE, row), past['cache_v_diff'].reshape(-1, PAGE, row),
                          wts['diff_lambda'][i], wts['diff_subln_g'][i], lam_init, i * n_pool)
        o_d = sb_sample(pt, proj3[:, :, 3 * row:4 * row], _pad_rows(kd3, PAGE), _pad_rows(vd3, PAGE),
                        past['cache_k_sb'].reshape(-1, PAGE, row), past['cache_v_sb'].reshape(-1, PAGE, row),
                        i * n_pool)
    m_out = (o_c.reshape(b * t, row), o_d.reshape(b * t, row))
    rows = (kcn3.reshape(b, t, H, DH), vc3.reshape(b, t, H, DH), kd3.reshape(b, t, H, DH),
            vd3.reshape(b, t, H, DH))
    return m_out, rows


def _trunk(x, mods, wts, past):
    b, t, d = x.shape
    depth = wts['ffn_w_up'].shape[0]
    x2 = x.reshape(b * t, d)
    per_token = past is not None
    tm, tn = _tiles(b, t, per_token)
    even_rows, odd_rows, ffn_rows = [], [], []
    for l in range(depth):
        i = l // 2
        mod = mods[l]
        if l % 2 == 0:
            (a1, a2), rows = _even_layer(x2, b, t, mod, wts, i, past)
            even_rows.append(rows)
            w_out = wts['w_out_even'][i]
        else:
            (a1, a2), rows = _odd_layer(x2, b, t, mod, wts, i, l, past)
            odd_rows.append(rows)
            w_out = wts['w_out_odd'][i]
        x2 = matmul_gated_residual([a1, a2], w_out, x2, _mod_arrays(mod, 2, t, per_token), t, tm, tn)
        u = norm_mod_matmul(x2, wts['norm_g'][l, 1], _mod_arrays(mod, 4, t, per_token),
                            _mod_arrays(mod, 3, t, per_token), wts['ffn_w_up'][l], t, tm, tn)
        u3 = u.reshape(b, t, -1)
        if past is None:
            ffn_buf = jnp.zeros((b, FFN_CONV_W - 1, u3.shape[2]), F32)
        else:
            ffn_buf = past['state_conv_ffn'][l]
        act = ffn_act(u3, ffn_buf, wts['ffn_conv_w'][l], wts['ffn_conv_b'][l], 1 if t > SUBLANES else b, 512)
        ffn_rows.append(u3[:, t - (FFN_CONV_W - 1):])
        x2 = matmul_gated_residual([act.reshape(b * t, -1)], wts['ffn_w_down'][l], x2,
                                   _mod_arrays(mod, 5, t, per_token), t, tm, tn)
    stack = lambda rows_list, k: jnp.stack([r[k] for r in rows_list])
    state = [stack(even_rows, k) for k in range(5)] + [stack(odd_rows, k) for k in range(4)] + [jnp.stack(ffn_rows)]
    return x2.reshape(b, t, d), state


def _permute_even_in(w):
    row = H * DH
    c_small = A_CONV + row
    c_qb = c_small + 2 * H
    c_fb = c_qb + 3 * row
    parts = [w[..., :c_small], w[..., c_qb:c_fb], w[..., c_small:c_qb], w[..., c_fb:c_fb + H]]
    pad = EVEN_N - sum(p.shape[-1] for p in parts)
    return jnp.pad(jnp.concatenate(parts, axis=-1), ((0, 0), (0, 0), (0, pad)))


def kernel(x_prompt, x_sample, cache_k_fox, cache_v_fox, cache_logf_fox, cache_k_diff, cache_v_diff, cache_k_sb, cache_v_sb, state_delta, state_conv_delta, state_conv_ffn, page_table, c_prompt, c_sample, w_ada, b_ada, norm_g, w_in_even, w_out_even, delta_conv_w, delta_a_log, delta_dt_bias, delta_onorm_g, fox_qn_g, fox_kn_g, fox_f_bias, w_in_odd, w_out_odd, diff_qn_g, diff_kn_g, diff_lambda, diff_subln_g, rel_bias, ffn_w_up, ffn_conv_w, ffn_conv_b, ffn_w_down):
    bp, d = c_prompt.shape
    bs = c_sample.shape[0]
    n_even = w_in_even.shape[0]
    zeros8 = jnp.zeros((n_even, H), F32)
    lane_pad = jnp.zeros((n_even, LANES - 3 * H), F32)
    wts = {
        'norm_g': norm_g,
        'w_in_even': _permute_even_in(w_in_even).astype(BF16),
        'w_out_even': w_out_even.astype(BF16),
        'delta_conv_w': delta_conv_w,
        'gate_bias': jnp.concatenate([zeros8, delta_dt_bias, fox_f_bias, lane_pad], axis=-1)[:, None, :],
        'gate_alog': jnp.concatenate([zeros8, delta_a_log, zeros8, lane_pad], axis=-1)[:, None, :],
        'delta_onorm_g': delta_onorm_g, 'fox_qn_g': fox_qn_g, 'fox_kn_g': fox_kn_g,
        'w_in_odd': w_in_odd.astype(BF16), 'w_out_odd': w_out_odd.astype(BF16),
        'diff_qn_g': diff_qn_g, 'diff_kn_g': diff_kn_g, 'diff_lambda': diff_lambda,
        'diff_subln_g': diff_subln_g, 'rel_bias': rel_bias,
        'ffn_w_up': ffn_w_up.astype(BF16), 'ffn_conv_w': ffn_conv_w, 'ffn_conv_b': ffn_conv_b,
        'ffn_w_down': ffn_w_down.astype(BF16),
    }
    past = {
        'cache_k_fox': cache_k_fox, 'cache_v_fox': cache_v_fox,
        'cache_logf_fox_t': jnp.swapaxes(cache_logf_fox, 2, 3).reshape(-1, H, PAGE),
        'cache_k_diff': cache_k_diff, 'cache_v_diff': cache_v_diff, 'cache_k_sb': cache_k_sb,
        'cache_v_sb': cache_v_sb, 'state_delta': state_delta, 'state_conv_delta': state_conv_delta,
        'state_conv_ffn': state_conv_ffn, 'page_table': page_table,
    }
    n_c = bp + bs
    c_all = _pad_rows(jnp.concatenate([c_prompt, c_sample], axis=0)[None], -(-n_c // SUBLANES) * SUBLANES)[0]
    mod_all = ada_mod(c_all, w_ada, b_ada)
    mods_p = [mod_all[l, :bp].reshape(bp, 6, d) for l in range(mod_all.shape[0])]
    mods_s = [mod_all[l, bp:n_c].reshape(bs, 6, d) for l in range(mod_all.shape[0])]
    y_prompt, sp = _trunk(x_prompt, mods_p, wts, None)
    y_sample, ss = _trunk(x_sample, mods_s, wts, past)
    return (y_prompt, y_sample, *sp, *ss)
```

```python
import functools
import math

import numpy as np
import jax
import jax.numpy as jnp
from jax import lax
from jax.experimental import pallas as pl
from jax.experimental.pallas import tpu as pltpu

F32 = jnp.float32
BF16 = jnp.bfloat16
HI = lax.Precision.HIGHEST

LANES = 128
SUBLANES = 8
MIB = 1024 * 1024

EPS = 1e-6
H = 8
DH = 128
DQ_C = 64
SC_W = 4
FFN_CONV_W = 3
DELTA_CHUNK = 64
REL_BUCKETS = 32
REL_MAX_DIST = 128
PAGE = 128

A_CONV = 3 * H * DH
QKV_A, Z_A, Q_B, K_B, V_B, SMALL = 0, 3072, 4096, 5120, 6144, 7168
EVEN_N = 7680
BETA_L, G_L, LOGF_L = 0, 8, 16


def _t5_bucket_starts():
    exact = REL_BUCKETS // 2
    d = np.arange(REL_MAX_DIST + 1)
    far = exact + (np.log(np.maximum(d, 1).astype(np.float32) / exact)
                   / math.log(REL_MAX_DIST / exact) * (REL_BUCKETS - exact)).astype(np.int32)
    bucket = np.where(d < exact, d, np.minimum(far, REL_BUCKETS - 1))
    return [int(np.argmax(bucket == b)) for b in range(REL_BUCKETS)]


T5_STARTS = _t5_bucket_starts()


def _cparams(n_axes, vmem_mib):
    return pltpu.CompilerParams(dimension_semantics=("arbitrary",) * n_axes,
                                vmem_limit_bytes=vmem_mib * MIB)


def _nt(a, b, precision=None):
    return lax.dot_general(a, b, (((1,), (1,)), ((), ())), precision=precision,
                           preferred_element_type=F32)


def _softplus_neg_abs(z):
    return jnp.log1p(jnp.exp(-jnp.abs(z)))


def _t5_bias(dist, rb_ref, h):
    val = jnp.full(dist.shape, rb_ref[REL_BUCKETS - 1, h], F32)
    for b in range(REL_BUCKETS - 2, -1, -1):
        val = jnp.where(dist < T5_STARTS[b + 1], rb_ref[b, h], val)
    return val


def _ada_kernel(c_ref, w_ref, b_ref, o_ref):
    c = c_ref[...]
    a = (c * jax.nn.sigmoid(c)).astype(BF16)
    o_ref[0] = jnp.dot(a, w_ref[0].astype(BF16), preferred_element_type=F32) + b_ref[0]


def ada_mod(c_all, w_ada, b_ada):
    nl, d, n = w_ada.shape
    mp = c_all.shape[0]
    tn = 1024
    return pl.pallas_call(
        _ada_kernel,
        grid=(nl, n // tn),
        in_specs=[pl.BlockSpec((mp, d), lambda l, j: (0, 0)),
                  pl.BlockSpec((1, d, tn), lambda l, j: (l, 0, j)),
                  pl.BlockSpec((1, 1, tn), lambda l, j: (l, 0, j))],
        out_specs=pl.BlockSpec((1, mp, tn), lambda l, j: (l, 0, j)),
        out_shape=jax.ShapeDtypeStruct((nl, mp, n), F32),
        compiler_params=_cparams(2, 40),
        name="ada_mod",
    )(c_all, w_ada, b_ada.reshape(nl, 1, n))


def _nmm_kernel(x_ref, g_ref, sc_ref, sh_ref, w_ref, o_ref, h_scr):
    @pl.when(pl.program_id(1) == 0)
    def _():
        x = x_ref[...]
        ms = jnp.mean(x * x, axis=-1, keepdims=True)
        y = x * lax.rsqrt(ms + EPS) * g_ref[...]
        h_scr[...] = (y * (1.0 + sc_ref[0]) + sh_ref[0]).astype(BF16)

    o_ref[...] = jnp.dot(h_scr[...], w_ref[...], preferred_element_type=F32)


def norm_mod_matmul(x, g, sc, sh, w, seq_rows, tm, tn):
    m, d = x.shape
    n = w.shape[1]
    r = sc.shape[1]
    if r == 1:
        mod_idx = lambda i, j: ((i * tm) // seq_rows, 0, 0)
    else:
        mod_idx = lambda i, j: (i, 0, 0)
    return pl.pallas_call(
        _nmm_kernel,
        grid=(m // tm, n // tn),
        in_specs=[pl.BlockSpec((tm, d), lambda i, j: (i, 0)),
                  pl.BlockSpec((1, d), lambda i, j: (0, 0)),
                  pl.BlockSpec((1, r, d), mod_idx),
                  pl.BlockSpec((1, r, d), mod_idx),
                  pl.BlockSpec((d, tn), lambda i, j: (0, j))],
        out_specs=pl.BlockSpec((tm, tn), lambda i, j: (i, j)),
        out_shape=jax.ShapeDtypeStruct((m, n), F32),
        scratch_shapes=[pltpu.VMEM((tm, d), BF16)],
        compiler_params=_cparams(2, 48),
        name="norm_mod_matmul",
    )(x, g.reshape(1, d), sc, sh, w)


def _mmres_kernel(*refs, n_a):
    a_refs = refs[:n_a]
    w_refs = refs[n_a:2 * n_a]
    xres_ref, gate_ref, o_ref = refs[2 * n_a:]
    acc = jnp.dot(a_refs[0][...].astype(BF16), w_refs[0][...], preferred_element_type=F32)
    for a_ref, w_ref in zip(a_refs[1:], w_refs[1:]):
        acc = acc + jnp.dot(a_ref[...].astype(BF16), w_ref[...], preferred_element_type=F32)
    o_ref[...] = xres_ref[...] + gate_ref[0] * acc


def matmul_gated_residual(a_list, w, xres, gate, seq_rows, tm, tn):
    m, n = xres.shape
    n_a = len(a_list)
    kp = a_list[0].shape[1]
    r = gate.shape[1]
    if r == 1:
        mod_idx = lambda i, j: ((i * tm) // seq_rows, 0, 0)
    else:
        mod_idx = lambda i, j: (i, 0, 0)
    in_specs = [pl.BlockSpec((tm, kp), lambda i, j: (i, 0)) for _ in range(n_a)]
    in_specs += [pl.BlockSpec((kp, tn), functools.partial(lambda i, j, p: (p, j), p=p)) for p in range(n_a)]
    in_specs += [pl.BlockSpec((tm, tn), lambda i, j: (i, j)),
                 pl.BlockSpec((1, r, tn), lambda i, j: mod_idx(i, j)[:2] + (j,))]
    return pl.pallas_call(
        functools.partial(_mmres_kernel, n_a=n_a),
        grid=(m // tm, n // tn),
        in_specs=in_specs,
        out_specs=pl.BlockSpec((tm, tn), lambda i, j: (i, j)),
        out_shape=jax.ShapeDtypeStruct((m, n), F32),
        compiler_params=_cparams(2, 48),
        name="matmul_gated_residual",
    )(*a_list, *([w] * n_a), xres, gate)


def _conv_head(xh, buf_ref, w, width):
    nb = buf_ref.shape[0]
    c = xh.shape[-1]
    t = lax.broadcasted_iota(jnp.int32, xh.shape, 0) % SUBLANES
    y = xh * w[width - 1:width]
    for s in range(1, width):
        sh = pltpu.roll(xh, s, 0)
        for tt in range(s):
            k = tt + width - 1 - s
            brow = buf_ref[:, k:k + 1, :]
            bsel = jnp.broadcast_to(brow, (nb, SUBLANES, c)).reshape(nb * SUBLANES, c)
            sh = jnp.where(t == tt, bsel, sh)
        y = y + sh * w[width - 1 - s:width - s]
    return y


def _conv_bulk(x2, w, width):
    y = x2 * w[width - 1:width]
    for s in range(1, width):
        y = y + pltpu.roll(x2, s, 0) * w[width - 1 - s:width - s]
    return y


def _conv_apply(x_ref, buf_ref, w_ref, t_len, width, emit):
    nb = x_ref.shape[0]
    c = x_ref.shape[2]
    w = w_ref[...]
    if t_len == SUBLANES:
        xh = x_ref[...].reshape(nb * SUBLANES, c)
        emit(None, _conv_head(xh, buf_ref, w, width))
    else:
        assert nb == 1
        emit(None, _conv_bulk(x_ref[0], w, width))
        emit(SUBLANES, _conv_head(x_ref[0, 0:SUBLANES, :], buf_ref, w, width))


def _dconv_kernel(x_ref, buf_ref, w_ref, o_ref, *, t_len):
    j = pl.program_id(1)
    nb = x_ref.shape[0]

    def emit(rows, y):
        y = y * jax.nn.sigmoid(y)
        nrm = y * lax.rsqrt(jnp.sum(y * y, axis=-1, keepdims=True) + EPS)
        y = jnp.where(j < H, nrm * (DH ** -0.5), jnp.where(j < 2 * H, nrm, y))
        if rows is None:
            o_ref[...] = y.reshape(o_ref.shape)
        else:
            o_ref[0, 0:rows, :] = y

    _conv_apply(x_ref, buf_ref, w_ref, t_len, SC_W, emit)


def delta_conv(proj3, buf, w, nb):
    b, t, _ = proj3.shape
    return pl.pallas_call(
        functools.partial(_dconv_kernel, t_len=t),
        grid=(b // nb, A_CONV // LANES),
        in_specs=[pl.BlockSpec((nb, t, LANES), lambda i, j: (i, 0, j)),
                  pl.BlockSpec((nb, SC_W - 1, LANES), lambda i, j: (i, 0, j)),
                  pl.BlockSpec((SC_W, LANES), lambda i, j: (0, j))],
        out_specs=pl.BlockSpec((nb, t, LANES), lambda i, j: (i, 0, j)),
        out_shape=jax.ShapeDtypeStruct((b, t, A_CONV), F32),
        compiler_params=_cparams(2, 32),
        name="delta_conv",
    )(proj3, buf, w)


def _ffn_act_kernel(xg_ref, xu_ref, bg_ref, bu_ref, wg_ref, wu_ref, cg_ref, cu_ref, o_ref, *, t_len):
    outs = {}

    def emit_g(rows, y):
        outs[("g", rows)] = y + cg_ref[...]

    def emit_u(rows, y):
        outs[("u", rows)] = y + cu_ref[...]

    _conv_apply(xg_ref, bg_ref, wg_ref, t_len, FFN_CONV_W, emit_g)
    _conv_apply(xu_ref, bu_ref, wu_ref, t_len, FFN_CONV_W, emit_u)
    for (kind, rows), gate in outs.items():
        if kind != "g":
            continue
        y = (gate * jax.nn.sigmoid(gate) * outs[("u", rows)]).astype(o_ref.dtype)
        if rows is None:
            o_ref[...] = y.reshape(o_ref.shape)
        else:
            o_ref[0, 0:rows, :] = y


def ffn_act(u3, buf, w, bias, nb, tc):
    b, t, f2 = u3.shape
    f = f2 // 2
    nj = f // tc
    bias2 = bias.reshape(1, f2)
    return pl.pallas_call(
        functools.partial(_ffn_act_kernel, t_len=t),
        grid=(b // nb, nj),
        in_specs=[pl.BlockSpec((nb, t, tc), lambda i, j: (i, 0, j)),
                  pl.BlockSpec((nb, t, tc), lambda i, j: (i, 0, j + nj)),
                  pl.BlockSpec((nb, FFN_CONV_W - 1, tc), lambda i, j: (i, 0, j)),
                  pl.BlockSpec((nb, FFN_CONV_W - 1, tc), lambda i, j: (i, 0, j + nj)),
                  pl.BlockSpec((FFN_CONV_W, tc), lambda i, j: (0, j)),
                  pl.BlockSpec((FFN_CONV_W, tc), lambda i, j: (0, j + nj)),
                  pl.BlockSpec((1, tc), lambda i, j: (0, j)),
                  pl.BlockSpec((1, tc), lambda i, j: (0, j + nj))],
        out_specs=pl.BlockSpec((nb, t, tc), lambda i, j: (i, 0, j)),
        out_shape=jax.ShapeDtypeStruct((b, t, f), F32),
        compiler_params=_cparams(2, 48),
        name="ffn_act",
    )(u3, u3, buf, buf, w, w, bias2, bias2)


def _hnorm_kernel(x_ref, g_ref, o_ref, *, group):
    x = x_ref[...]
    s = x * x
    if group == LANES:
        ms = jnp.mean(s, axis=-1, keepdims=True)
    else:
        lane = lax.broadcasted_iota(jnp.int32, x.shape, 1)
        lo = jnp.sum(jnp.where(lane < group, s, 0.0), axis=-1, keepdims=True)
        hi = jnp.sum(jnp.where(lane >= group, s, 0.0), axis=-1, keepdims=True)
        ms = jnp.where(lane < group, lo, hi) * (1.0 / group)
    o_ref[...] = x * lax.rsqrt(ms + EPS) * g_ref[...]


def head_norm(proj, col0, ncols, gain, tm):
    m = proj.shape[0]
    group = gain.shape[0]
    g = jnp.tile(gain, LANES // group).reshape(1, LANES)
    cb = col0 // LANES
    return pl.pallas_call(
        functools.partial(_hnorm_kernel, group=group),
        grid=(m // tm, ncols // LANES),
        in_specs=[pl.BlockSpec((tm, LANES), lambda i, j: (i, cb + j)),
                  pl.BlockSpec((1, LANES), lambda i, j: (0, 0))],
        out_specs=pl.BlockSpec((tm, LANES), lambda i, j: (i, j)),
        out_shape=jax.ShapeDtypeStruct((m, ncols), F32),
        compiler_params=_cparams(2, 32),
        name="head_norm",
    )(proj, g)


def _onorm_kernel(o_ref, z_ref, g_ref, y_ref):
    o = o_ref[...]
    z = z_ref[...]
    y = o * lax.rsqrt(jnp.mean(o * o, axis=-1, keepdims=True) + EPS) * g_ref[...]
    y_ref[...] = y * (z * jax.nn.sigmoid(z))


def onorm_gate(o, proj, gain, tm):
    m = o.shape[0]
    zb = Z_A // LANES
    return pl.pallas_call(
        _onorm_kernel,
        grid=(m // tm, H),
        in_specs=[pl.BlockSpec((tm, LANES), lambda i, j: (i, j)),
                  pl.BlockSpec((tm, LANES), lambda i, j: (i, zb + j)),
                  pl.BlockSpec((1, LANES), lambda i, j: (0, 0))],
        out_specs=pl.BlockSpec((tm, LANES), lambda i, j: (i, j)),
        out_shape=jax.ShapeDtypeStruct((m, H * DH), F32),
        compiler_params=_cparams(2, 32),
        name="onorm_gate",
    )(o, proj, gain.reshape(1, LANES))


def _gates_kernel(s_ref, bias_ref, alog_ref, g_ref, c_ref, carry, *, tm):
    @pl.when(pl.program_id(1) == 0)
    def _():
        carry[...] = jnp.zeros_like(carry)

    t = s_ref[0] + bias_ref[...]
    lane = lax.broadcasted_iota(jnp.int32, t.shape, 1)
    row = lax.broadcasted_iota(jnp.int32, t.shape, 0)
    l1p = _softplus_neg_abs(t)
    sig = jax.nn.sigmoid(t)
    g = -jnp.exp(alog_ref[...]) * (jnp.maximum(t, 0.0) + l1p)
    lsig = jnp.minimum(t, 0.0) - l1p
    gt = jnp.where(lane < G_L, sig, jnp.where(lane < LOGF_L, g, lsig))
    gt = jnp.where(lane < LOGF_L + H, gt, 0.0)
    g_ref[0] = gt
    cs = gt
    k = 1
    while k < tm:
        cs = cs + jnp.where(row >= k, pltpu.roll(cs, k, 0), 0.0)
        k *= 2
    cs = cs + carry[0:1, :]
    c_ref[0] = cs
    carry[...] = jnp.broadcast_to(cs[tm - 1:tm, :], carry.shape)


def even_gates(proj3, bias_vec, alog_vec, tm):
    b, t, _ = proj3.shape
    sb = SMALL // LANES
    return pl.pallas_call(
        functools.partial(_gates_kernel, tm=tm),
        grid=(b, t // tm),
        in_specs=[pl.BlockSpec((1, tm, LANES), lambda i, j: (i, j, sb)),
                  pl.BlockSpec((1, LANES), lambda i, j: (0, 0)),
                  pl.BlockSpec((1, LANES), lambda i, j: (0, 0))],
        out_specs=[pl.BlockSpec((1, tm, LANES), lambda i, j: (i, j, 0)),
                   pl.BlockSpec((1, tm, LANES), lambda i, j: (i, j, 0))],
        out_shape=[jax.ShapeDtypeStruct((b, t, LANES), F32),
                   jax.ShapeDtypeStruct((b, t, LANES), F32)],
        scratch_shapes=[pltpu.VMEM((SUBLANES, LANES), F32)],
        compiler_params=_cparams(2, 32),
        name="even_gates",
    )(proj3, bias_vec, alog_vec)


def _bmm(a, b, spec):
    return jnp.einsum(spec, a, b, precision=HI, preferred_element_type=F32)


def _delta_kernel(q_ref, k_ref, v_ref, gt_ref, s0_ref, o_ref, sn_ref,
                  wk_s, u0_s, qg_s, kd_s, qk_s, eg_s, *, t_len):
    L = DELTA_CHUNK
    nc = t_len // L
    h = pl.program_id(1)
    gt = gt_ref[0]
    lane = lax.broadcasted_iota(jnp.int32, gt.shape, 1)
    pos = lax.broadcasted_iota(jnp.int32, gt.shape, 0) % L
    onehot = (lane == G_L + h).astype(F32)
    cs = gt * onehot
    step = 1
    while step < L:
        cs = cs + jnp.where(pos >= step, pltpu.roll(cs, step, 0), 0.0)
        step *= 2
    q3 = q_ref[0].reshape(nc, L, DH)
    k3 = k_ref[0].reshape(nc, L, DH)
    v3 = v_ref[0].reshape(nc, L, DH)
    gt3 = gt.reshape(nc, L, LANES)
    cs3 = cs.reshape(nc, L, LANES)
    e3 = onehot.reshape(nc, L, LANES)
    lane3 = lax.broadcasted_iota(jnp.int32, gt3.shape, 2)
    beta = jnp.sum(jnp.where(lane3 == BETA_L + h, gt3, 0.0), axis=-1, keepdims=True)
    g3 = jnp.sum(cs3, axis=-1, keepdims=True)
    diff = _bmm(cs3, e3, 'cil,cjl->cij') - _bmm(e3, cs3, 'cil,cjl->cij')
    ii = lax.broadcasted_iota(jnp.int32, (nc, L, L), 1)
    jj = lax.broadcasted_iota(jnp.int32, (nc, L, L), 2)
    decay = jnp.where(ii >= jj, jnp.exp(jnp.where(ii >= jj, diff, 0.0)), 0.0)
    kb3 = k3 * beta
    a = jnp.where(ii > jj, _bmm(kb3, k3, 'cid,cjd->cij') * decay, 0.0)
    pk = -a
    tinv = (ii == jj).astype(F32) + pk
    span = 2
    while span < L:
        pk = _bmm(pk, pk, 'cij,cjk->cik')
        tinv = tinv + _bmm(tinv, pk, 'cij,cjk->cik')
        span *= 2
    eg = jnp.exp(g3)
    u0_s[...] = _bmm(tinv, v3 * beta, 'cij,cjd->cid')
    wk_s[...] = _bmm(tinv, kb3 * eg, 'cij,cjd->cid')
    qk_s[...] = _bmm(q3, k3, 'cid,cjd->cij') * decay
    glast = g3[:, L - 1:L, :]
    kd_s[...] = k3 * jnp.exp(glast - g3)
    qg_s[...] = q3 * eg
    eg_s[...] = jnp.broadcast_to(jnp.exp(glast), (nc, 1, LANES))

    def body(c, s):
        v_new = u0_s[c] - jnp.dot(wk_s[c], s, precision=HI, preferred_element_type=F32)
        o = (jnp.dot(qg_s[c], s, precision=HI, preferred_element_type=F32)
             + jnp.dot(qk_s[c], v_new, precision=HI, preferred_element_type=F32))
        o_ref[0, pl.ds(pl.multiple_of(c * L, L), L), :] = o
        return s * eg_s[c] + lax.dot_general(kd_s[c], v_new, (((0,), (0,)), ((), ())),
                                             precision=HI, preferred_element_type=F32)

    sn_ref[0, 0] = lax.fori_loop(0, nc, body, s0_ref[0, 0])


def delta_rule(qkv3, gates, s0):
    b, t, _ = qkv3.shape
    nc = t // DELTA_CHUNK
    L = DELTA_CHUNK
    return pl.pallas_call(
        functools.partial(_delta_kernel, t_len=t),
        grid=(b, H),
        in_specs=[pl.BlockSpec((1, t, DH), lambda i, j: (i, 0, j)),
                  pl.BlockSpec((1, t, DH), lambda i, j: (i, 0, H + j)),
                  pl.BlockSpec((1, t, DH), lambda i, j: (i, 0, 2 * H + j)),
                  pl.BlockSpec((1, t, LANES), lambda i, j: (i, 0, 0)),
                  pl.BlockSpec((1, 1, DH, DH), lambda i, j: (i, j, 0, 0))],
        out_specs=[pl.BlockSpec((1, t, DH), lambda i, j: (i, 0, j)),
                   pl.BlockSpec((1, 1, DH, DH), lambda i, j: (i, j, 0, 0))],
        out_shape=[jax.ShapeDtypeStruct((b, t, H * DH), F32),
                   jax.ShapeDtypeStruct((b, H, DH, DH), F32)],
        scratch_shapes=[pltpu.VMEM((nc, L, DH), F32), pltpu.VMEM((nc, L, DH), F32),
                        pltpu.VMEM((nc, L, DH), F32), pltpu.VMEM((nc, L, DH), F32),
                        pltpu.VMEM((nc, L, L), F32), pltpu.VMEM((nc, 1, LANES), F32)],
        compiler_params=_cparams(2, 48),
        name="delta_rule",
    )(qkv3, qkv3, qkv3, gates, s0)


def _last_kblock(qi, tq, tk):
    return ((qi + 1) * tq - 1) // tk


def _online_softmax(s, v_bf, m_ref, l_ref, acc_ref):
    m_prev = m_ref[...]
    m_new = jnp.maximum(m_prev, jnp.max(s, axis=-1, keepdims=True))
    alpha = jnp.exp(m_prev - m_new)
    p = jnp.exp(s - m_new)
    l_ref[...] = alpha * l_ref[...] + jnp.sum(p, axis=-1, keepdims=True)
    acc_ref[...] = alpha * acc_ref[...] + jnp.dot(p.astype(BF16), v_bf, preferred_element_type=F32)
    m_ref[...] = m_new


def _fox_p_kernel(q_ref, k_ref, v_ref, cq_ref, ck_ref, o_ref, m_s, l_s, acc_s, *, tq, tk):
    h = pl.program_id(1)
    qi = pl.program_id(2)
    ki = pl.program_id(3)
    last = _last_kblock(qi, tq, tk)

    @pl.when(ki == 0)
    def _():
        m_s[...] = jnp.full(m_s.shape, -jnp.inf, F32)
        l_s[...] = jnp.zeros_like(l_s)
        acc_s[...] = jnp.zeros_like(acc_s)

    def step(masked):
        s = _nt(q_ref[0].astype(BF16), k_ref[0].astype(BF16)) * (DH ** -0.5)
        cqb = cq_ref[0]
        lane = lax.broadcasted_iota(jnp.int32, cqb.shape, 1)
        cq = jnp.sum(jnp.where(lane == LOGF_L + h, cqb, 0.0), axis=-1, keepdims=True)
        s = s + cq - ck_ref[0, 0]
        if masked:
            qpos = qi * tq + lax.broadcasted_iota(jnp.int32, s.shape, 0)
            kpos = ki * tk + lax.broadcasted_iota(jnp.int32, s.shape, 1)
            s = jnp.where(kpos <= qpos, s, -jnp.inf)
        _online_softmax(s, v_ref[0].astype(BF16), m_s, l_s, acc_s)

    straddles = (ki + 1) * tk - 1 > qi * tq

    @pl.when(jnp.logical_and(ki <= last, straddles))
    def _():
        step(True)

    @pl.when(jnp.logical_and(ki <= last, jnp.logical_not(straddles)))
    def _():
        step(False)

    @pl.when(ki == last)
    def _():
        o_ref[0] = acc_s[...] / l_s[...]


def fox_prompt(qn3, kn3, proj3, cum, cum_t, tq, tk):
    b, t, _ = qn3.shape
    vb = V_B // LANES
    kidx = lambda bi, h, qi, ki: jnp.minimum(ki, _last_kblock(qi, tq, tk))
    return pl.pallas_call(
        functools.partial(_fox_p_kernel, tq=tq, tk=tk),
        grid=(b, H, t // tq, t // tk),
        in_specs=[pl.BlockSpec((1, tq, DH), lambda bi, h, qi, ki: (bi, qi, h)),
                  pl.BlockSpec((1, tk, DH), lambda bi, h, qi, ki: (bi, kidx(bi, h, qi, ki), h)),
                  pl.BlockSpec((1, tk, DH), lambda bi, h, qi, ki: (bi, kidx(bi, h, qi, ki), vb + h)),
                  pl.BlockSpec((1, tq, LANES), lambda bi, h, qi, ki: (bi, qi, 0)),
                  pl.BlockSpec((1, 1, 1, tk), lambda bi, h, qi, ki: (bi, h, 0, kidx(bi, h, qi, ki)))],
        out_specs=pl.BlockSpec((1, tq, DH), lambda bi, h, qi, ki: (bi, qi, h)),
        out_shape=jax.ShapeDtypeStruct((b, t, H * DH), F32),
        scratch_shapes=[pltpu.VMEM((tq, 1), F32), pltpu.VMEM((tq, 1), F32), pltpu.VMEM((tq, DH), F32)],
        compiler_params=_cparams(4, 32),
        name="fox_prompt",
    )(qn3, kn3, proj3, cum, cum_t)


def _t5_tile_kernel(rb_ref, o_ref, *, tb):
    h = pl.program_id(0)
    off = pl.program_id(1)
    dist = (off * tb + lax.broadcasted_iota(jnp.int32, (tb, tb), 0)
            - lax.broadcasted_iota(jnp.int32, (tb, tb), 1))
    o_ref[0, 0] = _t5_bias(dist, rb_ref, h)


def t5_tiles(rel_bias, tb):
    return pl.pallas_call(
        functools.partial(_t5_tile_kernel, tb=tb),
        grid=(H, 2),
        in_specs=[pl.BlockSpec(memory_space=pltpu.SMEM)],
        out_specs=pl.BlockSpec((1, 1, tb, tb), lambda h, o: (h, o, 0, 0)),
        out_shape=jax.ShapeDtypeStruct((H, 2, tb, tb), F32),
        compiler_params=_cparams(2, 32),
        name="t5_tiles",
    )(rel_bias)


def _diff_lambda(lam_ref, lam_init):
    lp = lam_ref[...]
    s01 = jnp.sum(lp[0:1] * lp[1:2], axis=-1, keepdims=True)
    s23 = jnp.sum(lp[2:3] * lp[3:4], axis=-1, keepdims=True)
    return jnp.exp(s01) - jnp.exp(s23) + lam_init


def _diff_finish(acc0, l0, acc1, l1, lam, g, lam_init):
    o = acc0 / l0 - lam * (acc1 / l1)
    return o * lax.rsqrt(jnp.mean(o * o, axis=-1, keepdims=True) + EPS) * g * (1.0 - lam_init)


def _split_halves(qf):
    lane = lax.broadcasted_iota(jnp.int32, qf.shape, 1)
    return jnp.concatenate([jnp.where(lane < DQ_C, qf, 0.0), jnp.where(lane >= DQ_C, qf, 0.0)], axis=0)


def _diff_p_kernel(rb_ref, q_ref, k_ref, v_ref, bias_ref, lam_ref, g_ref, o_ref, m_s, l_s, acc_s,
                   *, tb, lam_init):
    h = pl.program_id(1)
    qi = pl.program_id(2)
    ki = pl.program_id(3)

    @pl.when(ki == 0)
    def _():
        m_s[...] = jnp.full(m_s.shape, -jnp.inf, F32)
        l_s[...] = jnp.zeros_like(l_s)
        acc_s[...] = jnp.zeros_like(acc_s)

    def step(on_diagonal):
        q2 = _split_halves(q_ref[0]).astype(BF16)
        near = (qi - ki) <= 1
        bias = jnp.where(near, bias_ref[0, 0], rb_ref[REL_BUCKETS - 1, h])
        s = _nt(q2, k_ref[0].astype(BF16)) * (DQ_C ** -0.5) + jnp.concatenate([bias, bias], axis=0)
        if on_diagonal:
            row = lax.broadcasted_iota(jnp.int32, s.shape, 0)
            qpos = jnp.where(row >= tb, row - tb, row)
            s = jnp.where(lax.broadcasted_iota(jnp.int32, s.shape, 1) <= qpos, s, -jnp.inf)
        _online_softmax(s, v_ref[0].astype(BF16), m_s, l_s, acc_s)

    @pl.when(ki < qi)
    def _():
        step(False)

    @pl.when(ki == qi)
    def _():
        step(True)
        lam = _diff_lambda(lam_ref, lam_init)
        o_ref[0] = _diff_finish(acc_s[0:tb], l_s[0:tb], acc_s[tb:2 * tb], l_s[tb:2 * tb], lam, g_ref[...], lam_init)


def diff_prompt(qn3, kn3, proj3, v_col0, tiles, rel_bias, lam_p, subln_g, lam_init, tb):
    b, t, _ = qn3.shape
    vb = v_col0 // LANES
    kidx = lambda qi, ki: jnp.minimum(ki, qi)
    return pl.pallas_call(
        functools.partial(_diff_p_kernel, tb=tb, lam_init=lam_init),
        grid=(b, H, t // tb, t // tb),
        in_specs=[pl.BlockSpec(memory_space=pltpu.SMEM),
                  pl.BlockSpec((1, tb, DH), lambda bi, h, qi, ki: (bi, qi, h)),
                  pl.BlockSpec((1, tb, DH), lambda bi, h, qi, ki: (bi, kidx(qi, ki), h)),
                  pl.BlockSpec((1, tb, DH), lambda bi, h, qi, ki: (bi, kidx(qi, ki), vb + h)),
                  pl.BlockSpec((1, 1, tb, tb), lambda bi, h, qi, ki: (h, jnp.clip(qi - ki, 0, 1), 0, 0)),
                  pl.BlockSpec((4, DQ_C), lambda bi, h, qi, ki: (0, 0)),
                  pl.BlockSpec((1, DH), lambda bi, h, qi, ki: (0, 0))],
        out_specs=pl.BlockSpec((1, tb, DH), lambda bi, h, qi, ki: (bi, qi, h)),
        out_shape=jax.ShapeDtypeStruct((b, t, H * DH), F32),
        scratch_shapes=[pltpu.VMEM((2 * tb, 1), F32), pltpu.VMEM((2 * tb, 1), F32), pltpu.VMEM((2 * tb, DH), F32)],
        compiler_params=_cparams(4, 40),
        name="diff_prompt",
    )(rel_bias, qn3, kn3, proj3, tiles, lam_p, subln_g.reshape(1, DH))


def _suffix_matrix(n):
    return (lax.broadcasted_iota(jnp.int32, (n, n), 0) > lax.broadcasted_iota(jnp.int32, (n, n), 1)).astype(BF16)


def _rows_times_01(x, u, parts):
    m = x.shape[0]
    terms = []
    rest = x
    for _ in range(parts):
        term = rest.astype(BF16).astype(F32)
        terms.append(term)
        rest = rest - term
    y = jnp.dot(jnp.concatenate(terms, axis=0).astype(BF16), u, preferred_element_type=F32)
    out = y[0:m]
    for i in range(1, parts):
        out = out + y[i * m:(i + 1) * m]
    return out


SB_UNDERFLOW = -104.0


def _sb_logs(z):
    l1p = _softplus_neg_abs(z)
    return jnp.minimum(z, 0.0) - l1p, jnp.minimum(-z, 0.0) - l1p


def _sb_p_kernel(q_ref, k_ref, v_ref, o_ref, carry_s, acc_s, *, tq, tk):
    qi = pl.program_id(2)
    ki = pl.program_id(3)
    last = _last_kblock(qi, tq, tk)
    kb = last - ki

    @pl.when(ki == 0)
    def _():
        carry_s[...] = jnp.zeros_like(carry_s)
        acc_s[...] = jnp.zeros_like(acc_s)

    def step(masked):
        z = _nt(q_ref[0].astype(BF16), k_ref[0].astype(BF16)) * (DH ** -0.5)
        log_sig, log_keep = _sb_logs(z)
        if masked:
            qpos = qi * tq + lax.broadcasted_iota(jnp.int32, z.shape, 0)
            kpos = kb * tk + lax.broadcasted_iota(jnp.int32, z.shape, 1)
            mask = kpos < qpos
            log_keep = jnp.where(mask, log_keep, 0.0)
        later = carry_s[...] + _rows_times_01(log_keep, _suffix_matrix(tk), 2)
        a = jnp.exp(log_sig + later)
        if masked:
            a = jnp.where(mask, a, 0.0)
        acc_s[...] += jnp.dot(a.astype(BF16), v_ref[0].astype(BF16), preferred_element_type=F32)
        carry_s[...] += jnp.sum(log_keep, axis=-1, keepdims=True)

    live = jnp.logical_and(ki <= last, jnp.max(carry_s[...]) > SB_UNDERFLOW)
    straddles = (kb + 1) * tk > qi * tq

    @pl.when(jnp.logical_and(live, straddles))
    def _():
        step(True)

    @pl.when(jnp.logical_and(live, jnp.logical_not(straddles)))
    def _():
        step(False)

    @pl.when(ki == last)
    def _():
        o_ref[0] = acc_s[...]


def sb_prompt(proj3, q_col0, k_col0, v_col0, tq, tk):
    b, t, _ = proj3.shape
    qb, kb, vb = q_col0 // LANES, k_col0 // LANES, v_col0 // LANES
    kidx = lambda qi, ki: jnp.maximum(_last_kblock(qi, tq, tk) - ki, 0)
    return pl.pallas_call(
        functools.partial(_sb_p_kernel, tq=tq, tk=tk),
        grid=(b, H, t // tq, t // tk),
        in_specs=[pl.BlockSpec((1, tq, DH), lambda bi, h, qi, ki: (bi, qi, qb + h)),
                  pl.BlockSpec((1, tk, DH), lambda bi, h, qi, ki: (bi, kidx(qi, ki), kb + h)),
                  pl.BlockSpec((1, tk, DH), lambda bi, h, qi, ki: (bi, kidx(qi, ki), vb + h))],
        out_specs=pl.BlockSpec((1, tq, DH), lambda bi, h, qi, ki: (bi, qi, h)),
        out_shape=jax.ShapeDtypeStruct((b, t, H * DH), F32),
        scratch_shapes=[pltpu.VMEM((tq, 1), F32), pltpu.VMEM((tq, DH), F32)],
        compiler_params=_cparams(4, 32),
        name="sb_prompt",
    )(proj3, proj3, proj3)


PAGES_PER_STEP = 4


def _pages_per_step(n_pages):
    return max(d for d in range(1, PAGES_PER_STEP + 1) if n_pages % d == 0)


def _slot_page(pt_ref, b, p, slot, n_pages, pps, layer_off):
    return layer_off + pt_ref[b, n_pages - 1 - (jnp.maximum(p - 1, 0) * pps + slot)]


def _head_rows_bf16(refs, h):
    parts = [r[0, :, h, :].astype(BF16) for r in refs]
    return parts[0] if len(parts) == 1 else jnp.concatenate(parts, axis=0)


def _qk_heads(q_list, k_refs):
    return jnp.concatenate([_nt(q_list[h], _head_rows_bf16(k_refs, h)) for h in range(H)], axis=0)


def _pv_heads(p, v_refs, rows):
    return jnp.concatenate(
        [jnp.dot(p[h * rows:(h + 1) * rows].astype(BF16), _head_rows_bf16(v_refs, h), preferred_element_type=F32)
         for h in range(H)], axis=0)


def _softmax_step(s, v_refs, rows, m_s, l_s, acc_s):
    m_prev = m_s[...]
    m_new = jnp.maximum(m_prev, jnp.max(s, axis=-1, keepdims=True))
    alpha = jnp.exp(m_prev - m_new)
    p = jnp.exp(s - m_new)
    l_s[...] = alpha * l_s[...] + jnp.sum(p, axis=-1, keepdims=True)
    acc_s[...] = alpha * acc_s[...] + _pv_heads(p, v_refs, rows)
    m_s[...] = m_new


def _row_in_group(shape, group):
    return lax.broadcasted_iota(jnp.int32, shape, 0) % group


def _fox_s_kernel(pt_ref, q_ref, kn_ref, vn_ref, lfn_ref, cumn_ref, *rest, n_pages, pps, t_len):
    kp_refs, vp_refs, lfp_refs = rest[:pps], rest[pps:2 * pps], rest[2 * pps:3 * pps]
    o_ref, m_s, l_s, acc_s, carry_s = rest[3 * pps:]
    p = pl.program_id(1)
    u = _suffix_matrix(PAGE)
    q_list = [q_ref[0, :, h * DH:(h + 1) * DH].astype(BF16) for h in range(H)]

    def process(k_refs, v_refs, lf_refs, is_new):
        n = len(k_refs)
        lf = [r[0] for r in lf_refs]
        suf = _rows_times_01(lf[0] if n == 1 else jnp.concatenate(lf, axis=0), u, 3)
        off = -jnp.sum(lf[0], axis=-1, keepdims=True) if is_new else carry_s[...]
        cols = []
        for i in range(n):
            cols.append(suf[i * H:(i + 1) * H] + off)
            if not is_new:
                off = off + jnp.sum(lf[i], axis=-1, keepdims=True)
        kb = cols[0] if n == 1 else jnp.concatenate(cols, axis=1)
        bias = jnp.concatenate([jnp.broadcast_to(kb[h:h + 1], (t_len, n * PAGE)) for h in range(H)], axis=0)
        s = _qk_heads(q_list, k_refs) * (DH ** -0.5) + bias + cumn_ref[0]
        if is_new:
            s = jnp.where(lax.broadcasted_iota(jnp.int32, s.shape, 1) <= _row_in_group(s.shape, t_len), s, -jnp.inf)
        _softmax_step(s, v_refs, t_len, m_s, l_s, acc_s)
        if not is_new:
            carry_s[...] = off

    @pl.when(p == 0)
    def _():
        m_s[...] = jnp.full(m_s.shape, -jnp.inf, F32)
        l_s[...] = jnp.zeros_like(l_s)
        acc_s[...] = jnp.zeros_like(acc_s)
        carry_s[...] = jnp.zeros_like(carry_s)
        process([kn_ref], [vn_ref], [lfn_ref], True)

    @pl.when(p > 0)
    def _():
        process(kp_refs, vp_refs, lfp_refs, False)

    @pl.when(p == n_pages // pps)
    def _():
        o = acc_s[...] / l_s[...]
        for h in range(H):
            o_ref[0, :, h * DH:(h + 1) * DH] = o[h * t_len:(h + 1) * t_len]


def _pool_specs(block, n_pages, pps, layer_off):
    def idx(slot):
        return lambda b, p, pt: (_slot_page(pt, b, p, slot, n_pages, pps, layer_off),) + (0,) * (len(block) - 1)
    return [pl.BlockSpec(block, idx(slot)) for slot in range(pps)]


def _qkv_specs(t_len, n_pages, pps, layer_off):
    page_block = (1, PAGE, H, DH)
    specs = [pl.BlockSpec((1, t_len, H * DH), lambda b, p, pt: (b, 0, 0)),
             pl.BlockSpec(page_block, lambda b, p, pt: (b, 0, 0, 0)),
             pl.BlockSpec(page_block, lambda b, p, pt: (b, 0, 0, 0))]
    return (specs + _pool_specs(page_block, n_pages, pps, layer_off)
            + _pool_specs(page_block, n_pages, pps, layer_off))


def fox_sample(pt, qn3, kn4, vn4, lfn_t, cumn_col, kpool, vpool, lfpool_t, layer_off):
    b, t, row = qn3.shape
    n_pages = pt.shape[1]
    pps = _pages_per_step(n_pages)
    specs = _qkv_specs(t, n_pages, pps, layer_off)
    in_specs = specs[:3] + [pl.BlockSpec((1, H, PAGE), lambda bi, p, ptr: (bi, 0, 0)),
                            pl.BlockSpec((1, H * t, 1), lambda bi, p, ptr: (bi, 0, 0))]
    in_specs += specs[3:] + _pool_specs((1, H, PAGE), n_pages, pps, layer_off)
    return pl.pallas_call(
        functools.partial(_fox_s_kernel, n_pages=n_pages, pps=pps, t_len=t),
        grid_spec=pltpu.PrefetchScalarGridSpec(
            num_scalar_prefetch=1,
            grid=(b, n_pages // pps + 1),
            in_specs=in_specs,
            out_specs=pl.BlockSpec((1, t, row), lambda bi, p, ptr: (bi, 0, 0)),
            scratch_shapes=[pltpu.VMEM((H * t, 1), F32), pltpu.VMEM((H * t, 1), F32),
                            pltpu.VMEM((H * t, DH), F32), pltpu.VMEM((H, 1), F32)]),
        out_shape=jax.ShapeDtypeStruct((b, t, row), F32),
        compiler_params=_cparams(2, 40),
        name="fox_sample",
    )(pt, qn3, kn4, vn4, lfn_t, cumn_col, *([kpool] * pps), *([vpool] * pps), *([lfpool_t] * pps))


def _diff_s_kernel(pt_ref, rb_ref, q_ref, kn_ref, vn_ref, *rest, n_pages, pps, t_len, lam_init):
    kp_refs, vp_refs = rest[:pps], rest[pps:2 * pps]
    lam_ref, g_ref, o_ref, m_s, l_s, acc_s = rest[2 * pps:]
    p = pl.program_id(1)
    rows = 2 * t_len
    q_list = [_split_halves(q_ref[0, :, h * DH:(h + 1) * DH]).astype(BF16) for h in range(H)]
    tio = lax.broadcasted_iota(jnp.int32, (t_len, PAGE), 0)
    jio = lax.broadcasted_iota(jnp.int32, (t_len, PAGE), 1)
    row_head = lax.broadcasted_iota(jnp.int32, (H * rows, 1), 0) // rows
    far = jnp.zeros((H * rows, 1), F32)
    for h in range(H):
        far = jnp.where(row_head == h, rb_ref[REL_BUCKETS - 1, h], far)

    def near_bias(dist):
        tiles = [_t5_bias(dist, rb_ref, h) for h in range(H)]
        return jnp.concatenate([x for tile in tiles for x in (tile, tile)], axis=0)

    def process(k_refs, v_refs, mode):
        n = len(k_refs)
        s = _qk_heads(q_list, k_refs) * (DQ_C ** -0.5)
        if mode == "far":
            s = s + far
        else:
            bias = near_bias((tio - jio) if mode == "new" else (PAGE + tio - jio))
            if n > 1:
                bias = jnp.concatenate([bias, jnp.broadcast_to(far, (H * rows, (n - 1) * PAGE))], axis=1)
            s = s + bias
        if mode == "new":
            s = jnp.where(lax.broadcasted_iota(jnp.int32, s.shape, 1) <= _row_in_group(s.shape, t_len), s, -jnp.inf)
        _softmax_step(s, v_refs, rows, m_s, l_s, acc_s)

    @pl.when(p == 0)
    def _():
        m_s[...] = jnp.full(m_s.shape, -jnp.inf, F32)
        l_s[...] = jnp.zeros_like(l_s)
        acc_s[...] = jnp.zeros_like(acc_s)
        process([kn_ref], [vn_ref], "new")

    @pl.when(p == 1)
    def _():
        process(kp_refs, vp_refs, "last")

    @pl.when(p > 1)
    def _():
        process(kp_refs, vp_refs, "far")

    @pl.when(p == n_pages // pps)
    def _():
        lam = _diff_lambda(lam_ref, lam_init)
        acc = acc_s[...]
        l = l_s[...]
        for h in range(H):
            r0, r1, r2 = h * rows, h * rows + t_len, (h + 1) * rows
            o_ref[0, :, h * DH:(h + 1) * DH] = _diff_finish(acc[r0:r1], l[r0:r1], acc[r1:r2], l[r1:r2],
                                                            lam, g_ref[...], lam_init)


def diff_sample(pt, rel_bias, qn3, kn4, vn4, kpool, vpool, lam_p, subln_g, lam_init, layer_off):
    b, t, row = qn3.shape
    n_pages = pt.shape[1]
    pps = _pages_per_step(n_pages)
    in_specs = [pl.BlockSpec(memory_space=pltpu.SMEM)]
    in_specs += _qkv_specs(t, n_pages, pps, layer_off)
    in_specs += [pl.BlockSpec((4, DQ_C), lambda bi, p, ptr: (0, 0)),
                 pl.BlockSpec((1, DH), lambda bi, p, ptr: (0, 0))]
    return pl.pallas_call(
        functools.partial(_diff_s_kernel, n_pages=n_pages, pps=pps, t_len=t, lam_init=lam_init),
        grid_spec=pltpu.PrefetchScalarGridSpec(
            num_scalar_prefetch=1,
            grid=(b, n_pages // pps + 1),
            in_specs=in_specs,
            out_specs=pl.BlockSpec((1, t, row), lambda bi, p, ptr: (bi, 0, 0)),
            scratch_shapes=[pltpu.VMEM((2 * H * t, 1), F32), pltpu.VMEM((2 * H * t, 1), F32),
                            pltpu.VMEM((2 * H * t, DH), F32)]),
        out_shape=jax.ShapeDtypeStruct((b, t, row), F32),
        compiler_params=_cparams(2, 40),
        name="diff_sample",
    )(pt, rel_bias, qn3, kn4, vn4, *([kpool] * pps), *([vpool] * pps), lam_p, subln_g.reshape(1, DH))


def _sb_s_kernel(pt_ref, q_ref, kn_ref, vn_ref, *rest, n_pages, pps, t_len):
    kp_refs, vp_refs = rest[:pps], rest[pps:2 * pps]
    o_ref, carry_s, acc_s = rest[2 * pps:]
    p = pl.program_id(1)
    u = _suffix_matrix(PAGE)
    q_list = [q_ref[0, :, h * DH:(h + 1) * DH].astype(BF16) for h in range(H)]
    rows = H * t_len

    def process(k_refs, v_refs, is_new):
        n = len(k_refs)
        z = _qk_heads(q_list, k_refs) * (DH ** -0.5)
        log_sig, log_keep = _sb_logs(z)
        if is_new:
            mask = lax.broadcasted_iota(jnp.int32, z.shape, 1) < _row_in_group(z.shape, t_len)
            log_keep = jnp.where(mask, log_keep, 0.0)
        pages = [log_keep[:, i * PAGE:(i + 1) * PAGE] for i in range(n)]
        suf = _rows_times_01(pages[0] if n == 1 else jnp.concatenate(pages, axis=0), u, 2)
        off = carry_s[...]
        cols = []
        for i in range(n):
            cols.append(suf[i * rows:(i + 1) * rows] + off)
            off = off + jnp.sum(pages[i], axis=-1, keepdims=True)
        a = jnp.exp(log_sig + (cols[0] if n == 1 else jnp.concatenate(cols, axis=1)))
        if is_new:
            a = jnp.where(mask, a, 0.0)
        acc_s[...] += _pv_heads(a, v_refs, t_len)
        carry_s[...] = off

    @pl.when(p == 0)
    def _():
        carry_s[...] = jnp.zeros_like(carry_s)
        acc_s[...] = jnp.zeros_like(acc_s)
        process([kn_ref], [vn_ref], True)

    @pl.when(jnp.logical_and(p > 0, jnp.max(carry_s[...]) > SB_UNDERFLOW))
    def _():
        process(kp_refs, vp_refs, False)

    @pl.when(p == n_pages // pps)
    def _():
        acc = acc_s[...]
        for h in range(H):
            o_ref[0, :, h * DH:(h + 1) * DH] = acc[h * t_len:(h + 1) * t_len]


def sb_sample(pt, q3, kn4, vn4, kpool, vpool, layer_off):
    b, t, row = q3.shape
    n_pages = pt.shape[1]
    pps = _pages_per_step(n_pages)
    return pl.pallas_call(
        functools.partial(_sb_s_kernel, n_pages=n_pages, pps=pps, t_len=t),
        grid_spec=pltpu.PrefetchScalarGridSpec(
            num_scalar_prefetch=1,
            grid=(b, n_pages // pps + 1),
            in_specs=_qkv_specs(t, n_pages, pps, layer_off),
            out_specs=pl.BlockSpec((1, t, row), lambda bi, p, ptr: (bi, 0, 0)),
            scratch_shapes=[pltpu.VMEM((H * t, 1), F32), pltpu.VMEM((H * t, DH), F32)]),
        out_shape=jax.ShapeDtypeStruct((b, t, row), F32),
        compiler_params=_cparams(2, 40),
        name="sb_sample",
    )(pt, q3, kn4, vn4, *([kpool] * pps), *([vpool] * pps))


def _pad_rows(a, rows):
    return jnp.pad(a, ((0, 0), (0, rows - a.shape[1]), (0, 0)))


def _new_page(rows3):
    return _pad_rows(rows3, PAGE).reshape(rows3.shape[0], PAGE, H, DH)


def _pool_pages(cache):
    return cache.reshape((-1,) + cache.shape[2:])


def _mod_arrays(mod, idx, t, per_token):
    m = mod[:, idx]
    if per_token:
        return jnp.repeat(m, t, axis=0)[None]
    return m[:, None, :]


ROW_TILE = 512
COL_TILE = 512
GATE_TILE = 512
FOX_TILES = (512, 512)
DIFF_TILE = 512
SB_TILES = (512, 256)


def _tiles(b, t, per_token):
    return min(ROW_TILE, b * t if per_token else t), COL_TILE


def _even_layer(x2, b, t, mod, wts, i, past):
    per_token = past is not None
    tm, tn = _tiles(b, t, per_token)
    proj = norm_mod_matmul(x2, wts['norm_g'][2 * i, 0], _mod_arrays(mod, 1, t, per_token),
                           _mod_arrays(mod, 0, t, per_token), wts['w_in_even'][i], t, tm, tn)
    proj3 = proj.reshape(b, t, EVEN_N)
    if past is None:
        conv_buf = jnp.zeros((b, SC_W - 1, A_CONV), F32)
        s0 = jnp.zeros((b, H, DH, DH), F32)
    else:
        conv_buf = past['state_conv_delta'][i]
        s0 = past['state_delta'][i]
    qkv3 = delta_conv(proj3, conv_buf, wts['delta_conv_w'][i], 1 if t > SUBLANES else b)
    gates, cum = even_gates(proj3, wts['gate_bias'][i], wts['gate_alog'][i], min(t, GATE_TILE))
    t_pad = -(-t // DELTA_CHUNK) * DELTA_CHUNK
    o_a, s_new = delta_rule(_pad_rows(qkv3, t_pad), _pad_rows(gates, t_pad), s0)
    o_a = onorm_gate(o_a[:, :t].reshape(b * t, H * DH), proj, wts['delta_onorm_g'][i], tm)
    qn = head_norm(proj, Q_B, H * DH, wts['fox_qn_g'][i], tm)
    kn = head_norm(proj, K_B, H * DH, wts['fox_kn_g'][i], tm)
    qn3 = qn.reshape(b, t, H * DH)
    kn3 = kn.reshape(b, t, H * DH)
    v3 = proj3[:, :, V_B:V_B + H * DH]
    if past is None:
        cum_t = jnp.swapaxes(cum[:, :, LOGF_L:LOGF_L + H], 1, 2).reshape(b, H, 1, t)
        o_b = fox_prompt(qn3, kn3, proj3, cum, cum_t, min(t, FOX_TILES[0]), min(t, FOX_TILES[1]))
    else:
        pt = past['page_table']
        n_pool = past['cache_k_fox'].shape[1]
        lf_t = jnp.swapaxes(gates[:, :, LOGF_L:LOGF_L + H], 1, 2)
        lfn_t = jnp.pad(lf_t, ((0, 0), (0, 0), (0, PAGE - t)))
        cumn_col = jnp.swapaxes(cum[:, :, LOGF_L:LOGF_L + H], 1, 2).reshape(b, H * t, 1)
        o_b = fox_sample(pt, qn3, _new_page(kn3), _new_page(v3), lfn_t, cumn_col,
                         _pool_pages(past['cache_k_fox']), _pool_pages(past['cache_v_fox']),
                         past['cache_logf_fox_t'], i * n_pool)
    m_out = (o_a, o_b.reshape(b * t, H * DH))
    rows = (kn3.reshape(b, t, H, DH), v3.reshape(b, t, H, DH), gates[:, :, LOGF_L:LOGF_L + H], s_new,
            proj3[:, t - (SC_W - 1):, :A_CONV])
    return m_out, rows


def _odd_layer(x2, b, t, mod, wts, i, l, past):
    per_token = past is not None
    tm, tn = _tiles(b, t, per_token)
    row = H * DH
    proj = norm_mod_matmul(x2, wts['norm_g'][l, 0], _mod_arrays(mod, 1, t, per_token),
                           _mod_arrays(mod, 0, t, per_token), wts['w_in_odd'][i], t, tm, tn)
    proj3 = proj.reshape(b, t, 6 * row)
    lam_init = 0.8 - 0.6 * math.exp(-0.3 * l)
    qcn3 = head_norm(proj, 0, row, wts['diff_qn_g'][i], tm).reshape(b, t, row)
    kcn3 = head_norm(proj, row, row, wts['diff_kn_g'][i], tm).reshape(b, t, row)
    vc3 = proj3[:, :, 2 * row:3 * row]
    kd3 = proj3[:, :, 4 * row:5 * row]
    vd3 = proj3[:, :, 5 * row:6 * row]
    if past is None:
        tb = min(t, DIFF_TILE)
        tiles = t5_tiles(wts['rel_bias'], tb)
        o_c = diff_prompt(qcn3, kcn3, proj3, 2 * row, tiles, wts['rel_bias'], wts['diff_lambda'][i],
                          wts['diff_subln_g'][i], lam_init, tb)
        o_d = sb_prompt(proj3, 3 * row, 4 * row, 5 * row, min(t, SB_TILES[0]), min(t, SB_TILES[1]))
    else:
        pt = past['page_table']
        n_pool = past['cache_k_diff'].shape[1]
        o_c = diff_sample(pt, wts['rel_bias'], qcn3, _new_page(kcn3), _new_page(vc3),
                          _pool_pages(past['cache_k_diff']), _pool_pages(past['cache_v_diff']),
                          wts['diff_lambda'][i], wts['diff_subln_g'][i], lam_init, i * n_pool)
        o_d = sb_sample(pt, proj3[:, :, 3 * row:4 * row], _new_page(kd3), _new_page(vd3),
                        _pool_pages(past['cache_k_sb']), _pool_pages(past['cache_v_sb']), i * n_pool)
    m_out = (o_c.reshape(b * t, row), o_d.reshape(b * t, row))
    rows = (kcn3.reshape(b, t, H, DH), vc3.reshape(b, t, H, DH), kd3.reshape(b, t, H, DH),
            vd3.reshape(b, t, H, DH))
    return m_out, rows


def _trunk(x, mods, wts, past):
    b, t, d = x.shape
    depth = wts['ffn_w_up'].shape[0]
    x2 = x.reshape(b * t, d)
    per_token = past is not None
    tm, tn = _tiles(b, t, per_token)
    even_rows, odd_rows, ffn_rows = [], [], []
    for l in range(depth):
        i = l // 2
        mod = mods[l]
        if l % 2 == 0:
            (a1, a2), rows = _even_layer(x2, b, t, mod, wts, i, past)
            even_rows.append(rows)
            w_out = wts['w_out_even'][i]
        else:
            (a1, a2), rows = _odd_layer(x2, b, t, mod, wts, i, l, past)
            odd_rows.append(rows)
            w_out = wts['w_out_odd'][i]
        x2 = matmul_gated_residual([a1, a2], w_out, x2, _mod_arrays(mod, 2, t, per_token), t, tm, tn)
        u = norm_mod_matmul(x2, wts['norm_g'][l, 1], _mod_arrays(mod, 4, t, per_token),
                            _mod_arrays(mod, 3, t, per_token), wts['ffn_w_up'][l], t, tm, tn)
        u3 = u.reshape(b, t, -1)
        if past is None:
            ffn_buf = jnp.zeros((b, FFN_CONV_W - 1, u3.shape[2]), F32)
        else:
            ffn_buf = past['state_conv_ffn'][l]
        act = ffn_act(u3, ffn_buf, wts['ffn_conv_w'][l], wts['ffn_conv_b'][l], 1 if t > SUBLANES else b, 512)
        ffn_rows.append(u3[:, t - (FFN_CONV_W - 1):])
        x2 = matmul_gated_residual([act.reshape(b * t, -1)], wts['ffn_w_down'][l], x2,
                                   _mod_arrays(mod, 5, t, per_token), t, tm, tn)
    stack = lambda rows_list, k: jnp.stack([r[k] for r in rows_list])
    state = [stack(even_rows, k) for k in range(5)] + [stack(odd_rows, k) for k in range(4)] + [jnp.stack(ffn_rows)]
    return x2.reshape(b, t, d), state


def _permute_even_in(w):
    row = H * DH
    c_small = A_CONV + row
    c_qb = c_small + 2 * H
    c_fb = c_qb + 3 * row
    parts = [w[..., :c_small], w[..., c_qb:c_fb], w[..., c_small:c_qb], w[..., c_fb:c_fb + H]]
    pad = EVEN_N - sum(p.shape[-1] for p in parts)
    return jnp.pad(jnp.concatenate(parts, axis=-1), ((0, 0), (0, 0), (0, pad)))


def kernel(x_prompt, x_sample, cache_k_fox, cache_v_fox, cache_logf_fox, cache_k_diff, cache_v_diff, cache_k_sb, cache_v_sb, state_delta, state_conv_delta, state_conv_ffn, page_table, c_prompt, c_sample, w_ada, b_ada, norm_g, w_in_even, w_out_even, delta_conv_w, delta_a_log, delta_dt_bias, delta_onorm_g, fox_qn_g, fox_kn_g, fox_f_bias, w_in_odd, w_out_odd, diff_qn_g, diff_kn_g, diff_lambda, diff_subln_g, rel_bias, ffn_w_up, ffn_conv_w, ffn_conv_b, ffn_w_down):
    bp, d = c_prompt.shape
    bs = c_sample.shape[0]
    n_even = w_in_even.shape[0]
    zeros8 = jnp.zeros((n_even, H), F32)
    lane_pad = jnp.zeros((n_even, LANES - 3 * H), F32)
    wts = {
        'norm_g': norm_g,
        'w_in_even': _permute_even_in(w_in_even).astype(BF16),
        'w_out_even': w_out_even.astype(BF16),
        'delta_conv_w': delta_conv_w,
        'gate_bias': jnp.concatenate([zeros8, delta_dt_bias, fox_f_bias, lane_pad], axis=-1)[:, None, :],
        'gate_alog': jnp.concatenate([zeros8, delta_a_log, zeros8, lane_pad], axis=-1)[:, None, :],
        'delta_onorm_g': delta_onorm_g, 'fox_qn_g': fox_qn_g, 'fox_kn_g': fox_kn_g,
        'w_in_odd': w_in_odd.astype(BF16), 'w_out_odd': w_out_odd.astype(BF16),
        'diff_qn_g': diff_qn_g, 'diff_kn_g': diff_kn_g, 'diff_lambda': diff_lambda,
        'diff_subln_g': diff_subln_g, 'rel_bias': rel_bias,
        'ffn_w_up': ffn_w_up.astype(BF16), 'ffn_conv_w': ffn_conv_w, 'ffn_conv_b': ffn_conv_b,
        'ffn_w_down': ffn_w_down.astype(BF16),
    }
    past = {
        'cache_k_fox': cache_k_fox, 'cache_v_fox': cache_v_fox,
        'cache_logf_fox_t': jnp.swapaxes(cache_logf_fox, 2, 3).reshape(-1, H, PAGE),
        'cache_k_diff': cache_k_diff, 'cache_v_diff': cache_v_diff, 'cache_k_sb': cache_k_sb,
        'cache_v_sb': cache_v_sb, 'state_delta': state_delta, 'state_conv_delta': state_conv_delta,
        'state_conv_ffn': state_conv_ffn, 'page_table': page_table,
    }
    n_c = bp + bs
    c_all = _pad_rows(jnp.concatenate([c_prompt, c_sample], axis=0)[None], -(-n_c // SUBLANES) * SUBLANES)[0]
    mod_all = ada_mod(c_all, w_ada, b_ada)
    mods_p = [mod_all[l, :bp].reshape(bp, 6, d) for l in range(mod_all.shape[0])]
    mods_s = [mod_all[l, bp:n_c].reshape(bs, 6, d) for l in range(mod_all.shape[0])]
    y_prompt, sp = _trunk(x_prompt, mods_p, wts, None)
    y_sample, ss = _trunk(x_sample, mods_s, wts, past)
    return (y_prompt, y_sample, *sp, *ss)
```

```python
import functools
import math

import numpy as np
import jax
import jax.numpy as jnp
from jax import lax
from jax.experimental import pallas as pl
from jax.experimental.pallas import tpu as pltpu

F32 = jnp.float32
BF16 = jnp.bfloat16
HI = lax.Precision.HIGHEST

LANES = 128
SUBLANES = 8
MIB = 1024 * 1024

EPS = 1e-6
H = 8
DH = 128
DQ_C = 64
SC_W = 4
FFN_CONV_W = 3
DELTA_CHUNK = 64
REL_BUCKETS = 32
REL_MAX_DIST = 128
PAGE = 128

A_CONV = 3 * H * DH
QKV_A, Z_A, Q_B, K_B, V_B, SMALL = 0, 3072, 4096, 5120, 6144, 7168
EVEN_N = 7680
BETA_L, G_L, LOGF_L = 0, 8, 16


def _t5_bucket_starts():
    exact = REL_BUCKETS // 2
    d = np.arange(REL_MAX_DIST + 1)
    far = exact + (np.log(np.maximum(d, 1).astype(np.float32) / exact)
                   / math.log(REL_MAX_DIST / exact) * (REL_BUCKETS - exact)).astype(np.int32)
    bucket = np.where(d < exact, d, np.minimum(far, REL_BUCKETS - 1))
    return [int(np.argmax(bucket == b)) for b in range(REL_BUCKETS)]


T5_STARTS = _t5_bucket_starts()


def _cparams(n_axes, vmem_mib):
    return pltpu.CompilerParams(dimension_semantics=("arbitrary",) * n_axes,
                                vmem_limit_bytes=vmem_mib * MIB)


def _nt(a, b, precision=None):
    return lax.dot_general(a, b, (((1,), (1,)), ((), ())), precision=precision,
                           preferred_element_type=F32)


def _softplus_neg_abs(z):
    return jnp.log1p(jnp.exp(-jnp.abs(z)))


def _t5_bias(dist, rb_ref, h):
    val = jnp.full(dist.shape, rb_ref[REL_BUCKETS - 1, h], F32)
    for b in range(REL_BUCKETS - 2, -1, -1):
        val = jnp.where(dist < T5_STARTS[b + 1], rb_ref[b, h], val)
    return val


def _ada_kernel(c_ref, w_ref, b_ref, o_ref):
    c = c_ref[...]
    a = (c * jax.nn.sigmoid(c)).astype(BF16)
    o_ref[0] = jnp.dot(a, w_ref[0].astype(BF16), preferred_element_type=F32) + b_ref[0]


def ada_mod(c_all, w_ada, b_ada):
    nl, d, n = w_ada.shape
    mp = c_all.shape[0]
    tn = 1024
    return pl.pallas_call(
        _ada_kernel,
        grid=(nl, n // tn),
        in_specs=[pl.BlockSpec((mp, d), lambda l, j: (0, 0)),
                  pl.BlockSpec((1, d, tn), lambda l, j: (l, 0, j)),
                  pl.BlockSpec((1, 1, tn), lambda l, j: (l, 0, j))],
        out_specs=pl.BlockSpec((1, mp, tn), lambda l, j: (l, 0, j)),
        out_shape=jax.ShapeDtypeStruct((nl, mp, n), F32),
        compiler_params=_cparams(2, 40),
        name="ada_mod",
    )(c_all, w_ada, b_ada.reshape(nl, 1, n))


def _nmm_kernel(x_ref, g_ref, sc_ref, sh_ref, w_ref, o_ref, h_scr):
    @pl.when(pl.program_id(1) == 0)
    def _():
        x = x_ref[...]
        ms = jnp.mean(x * x, axis=-1, keepdims=True)
        y = x * lax.rsqrt(ms + EPS) * g_ref[...]
        h_scr[...] = (y * (1.0 + sc_ref[0]) + sh_ref[0]).astype(BF16)

    o_ref[...] = jnp.dot(h_scr[...], w_ref[...], preferred_element_type=F32)


def norm_mod_matmul(x, g, sc, sh, w, seq_rows, tm, tn):
    m, d = x.shape
    n = w.shape[1]
    r = sc.shape[1]
    if r == 1:
        mod_idx = lambda i, j: ((i * tm) // seq_rows, 0, 0)
    else:
        mod_idx = lambda i, j: (i, 0, 0)
    return pl.pallas_call(
        _nmm_kernel,
        grid=(m // tm, n // tn),
        in_specs=[pl.BlockSpec((tm, d), lambda i, j: (i, 0)),
                  pl.BlockSpec((1, d), lambda i, j: (0, 0)),
                  pl.BlockSpec((1, r, d), mod_idx),
                  pl.BlockSpec((1, r, d), mod_idx),
                  pl.BlockSpec((d, tn), lambda i, j: (0, j))],
        out_specs=pl.BlockSpec((tm, tn), lambda i, j: (i, j)),
        out_shape=jax.ShapeDtypeStruct((m, n), F32),
        scratch_shapes=[pltpu.VMEM((tm, d), BF16)],
        compiler_params=_cparams(2, 48),
        name="norm_mod_matmul",
    )(x, g.reshape(1, d), sc, sh, w)


def _mmres_kernel(*refs, n_a):
    a_refs = refs[:n_a]
    w_refs = refs[n_a:2 * n_a]
    xres_ref, gate_ref, o_ref = refs[2 * n_a:]
    acc = jnp.dot(a_refs[0][...].astype(BF16), w_refs[0][...], preferred_element_type=F32)
    for a_ref, w_ref in zip(a_refs[1:], w_refs[1:]):
        acc = acc + jnp.dot(a_ref[...].astype(BF16), w_ref[...], preferred_element_type=F32)
    o_ref[...] = xres_ref[...] + gate_ref[0] * acc


def matmul_gated_residual(a_list, w, xres, gate, seq_rows, tm, tn):
    m, n = xres.shape
    n_a = len(a_list)
    kp = a_list[0].shape[1]
    r = gate.shape[1]
    if r == 1:
        mod_idx = lambda i, j: ((i * tm) // seq_rows, 0, 0)
    else:
        mod_idx = lambda i, j: (i, 0, 0)
    in_specs = [pl.BlockSpec((tm, kp), lambda i, j: (i, 0)) for _ in range(n_a)]
    in_specs += [pl.BlockSpec((kp, tn), functools.partial(lambda i, j, p: (p, j), p=p)) for p in range(n_a)]
    in_specs += [pl.BlockSpec((tm, tn), lambda i, j: (i, j)),
                 pl.BlockSpec((1, r, tn), lambda i, j: mod_idx(i, j)[:2] + (j,))]
    return pl.pallas_call(
        functools.partial(_mmres_kernel, n_a=n_a),
        grid=(m // tm, n // tn),
        in_specs=in_specs,
        out_specs=pl.BlockSpec((tm, tn), lambda i, j: (i, j)),
        out_shape=jax.ShapeDtypeStruct((m, n), F32),
        compiler_params=_cparams(2, 48),
        name="matmul_gated_residual",
    )(*a_list, *([w] * n_a), xres, gate)


def _conv_head(xh, buf_ref, w, width):
    nb = buf_ref.shape[0]
    c = xh.shape[-1]
    t = lax.broadcasted_iota(jnp.int32, xh.shape, 0) % SUBLANES
    y = xh * w[width - 1:width]
    for s in range(1, width):
        sh = pltpu.roll(xh, s, 0)
        for tt in range(s):
            k = tt + width - 1 - s
            brow = buf_ref[:, k:k + 1, :]
            bsel = jnp.broadcast_to(brow, (nb, SUBLANES, c)).reshape(nb * SUBLANES, c)
            sh = jnp.where(t == tt, bsel, sh)
        y = y + sh * w[width - 1 - s:width - s]
    return y


def _conv_bulk(x2, w, width):
    y = x2 * w[width - 1:width]
    for s in range(1, width):
        y = y + pltpu.roll(x2, s, 0) * w[width - 1 - s:width - s]
    return y


def _conv_apply(x_ref, buf_ref, w_ref, t_len, width, emit):
    nb = x_ref.shape[0]
    c = x_ref.shape[2]
    w = w_ref[...]
    if t_len == SUBLANES:
        xh = x_ref[...].reshape(nb * SUBLANES, c)
        emit(None, _conv_head(xh, buf_ref, w, width))
    else:
        assert nb == 1
        emit(None, _conv_bulk(x_ref[0], w, width))
        emit(SUBLANES, _conv_head(x_ref[0, 0:SUBLANES, :], buf_ref, w, width))


def _dconv_kernel(x_ref, buf_ref, w_ref, o_ref, *, t_len):
    j = pl.program_id(1)
    nb = x_ref.shape[0]

    def emit(rows, y):
        y = y * jax.nn.sigmoid(y)
        nrm = y * lax.rsqrt(jnp.sum(y * y, axis=-1, keepdims=True) + EPS)
        y = jnp.where(j < H, nrm * (DH ** -0.5), jnp.where(j < 2 * H, nrm, y))
        if rows is None:
            o_ref[...] = y.reshape(o_ref.shape)
        else:
            o_ref[0, 0:rows, :] = y

    _conv_apply(x_ref, buf_ref, w_ref, t_len, SC_W, emit)


def delta_conv(proj3, buf, w, nb):
    b, t, _ = proj3.shape
    return pl.pallas_call(
        functools.partial(_dconv_kernel, t_len=t),
        grid=(b // nb, A_CONV // LANES),
        in_specs=[pl.BlockSpec((nb, t, LANES), lambda i, j: (i, 0, j)),
                  pl.BlockSpec((nb, SC_W - 1, LANES), lambda i, j: (i, 0, j)),
                  pl.BlockSpec((SC_W, LANES), lambda i, j: (0, j))],
        out_specs=pl.BlockSpec((nb, t, LANES), lambda i, j: (i, 0, j)),
        out_shape=jax.ShapeDtypeStruct((b, t, A_CONV), F32),
        compiler_params=_cparams(2, 32),
        name="delta_conv",
    )(proj3, buf, w)


def _ffn_act_kernel(xg_ref, xu_ref, bg_ref, bu_ref, wg_ref, wu_ref, cg_ref, cu_ref, o_ref, *, t_len):
    outs = {}

    def emit_g(rows, y):
        outs[("g", rows)] = y + cg_ref[...]

    def emit_u(rows, y):
        outs[("u", rows)] = y + cu_ref[...]

    _conv_apply(xg_ref, bg_ref, wg_ref, t_len, FFN_CONV_W, emit_g)
    _conv_apply(xu_ref, bu_ref, wu_ref, t_len, FFN_CONV_W, emit_u)
    for (kind, rows), gate in outs.items():
        if kind != "g":
            continue
        y = (gate * jax.nn.sigmoid(gate) * outs[("u", rows)]).astype(o_ref.dtype)
        if rows is None:
            o_ref[...] = y.reshape(o_ref.shape)
        else:
            o_ref[0, 0:rows, :] = y


def ffn_act(u3, buf, w, bias, nb, tc):
    b, t, f2 = u3.shape
    f = f2 // 2
    nj = f // tc
    bias2 = bias.reshape(1, f2)
    return pl.pallas_call(
        functools.partial(_ffn_act_kernel, t_len=t),
        grid=(b // nb, nj),
        in_specs=[pl.BlockSpec((nb, t, tc), lambda i, j: (i, 0, j)),
                  pl.BlockSpec((nb, t, tc), lambda i, j: (i, 0, j + nj)),
                  pl.BlockSpec((nb, FFN_CONV_W - 1, tc), lambda i, j: (i, 0, j)),
                  pl.BlockSpec((nb, FFN_CONV_W - 1, tc), lambda i, j: (i, 0, j + nj)),
                  pl.BlockSpec((FFN_CONV_W, tc), lambda i, j: (0, j)),
                  pl.BlockSpec((FFN_CONV_W, tc), lambda i, j: (0, j + nj)),
                  pl.BlockSpec((1, tc), lambda i, j: (0, j)),
                  pl.BlockSpec((1, tc), lambda i, j: (0, j + nj))],
        out_specs=pl.BlockSpec((nb, t, tc), lambda i, j: (i, 0, j)),
        out_shape=jax.ShapeDtypeStruct((b, t, f), F32),
        compiler_params=_cparams(2, 48),
        name="ffn_act",
    )(u3, u3, buf, buf, w, w, bias2, bias2)


def _hnorm_kernel(x_ref, g_ref, o_ref, *, group):
    x = x_ref[...]
    s = x * x
    if group == LANES:
        ms = jnp.mean(s, axis=-1, keepdims=True)
    else:
        lane = lax.broadcasted_iota(jnp.int32, x.shape, 1)
        lo = jnp.sum(jnp.where(lane < group, s, 0.0), axis=-1, keepdims=True)
        hi = jnp.sum(jnp.where(lane >= group, s, 0.0), axis=-1, keepdims=True)
        ms = jnp.where(lane < group, lo, hi) * (1.0 / group)
    o_ref[...] = x * lax.rsqrt(ms + EPS) * g_ref[...]


def head_norm(proj, col0, ncols, gain, tm):
    m = proj.shape[0]
    group = gain.shape[0]
    g = jnp.tile(gain, LANES // group).reshape(1, LANES)
    cb = col0 // LANES
    return pl.pallas_call(
        functools.partial(_hnorm_kernel, group=group),
        grid=(m // tm, ncols // LANES),
        in_specs=[pl.BlockSpec((tm, LANES), lambda i, j: (i, cb + j)),
                  pl.BlockSpec((1, LANES), lambda i, j: (0, 0))],
        out_specs=pl.BlockSpec((tm, LANES), lambda i, j: (i, j)),
        out_shape=jax.ShapeDtypeStruct((m, ncols), F32),
        compiler_params=_cparams(2, 32),
        name="head_norm",
    )(proj, g)


def _onorm_kernel(o_ref, z_ref, g_ref, y_ref):
    o = o_ref[...]
    z = z_ref[...]
    y = o * lax.rsqrt(jnp.mean(o * o, axis=-1, keepdims=True) + EPS) * g_ref[...]
    y_ref[...] = y * (z * jax.nn.sigmoid(z))


def onorm_gate(o, proj, gain, tm):
    m = o.shape[0]
    zb = Z_A // LANES
    return pl.pallas_call(
        _onorm_kernel,
        grid=(m // tm, H),
        in_specs=[pl.BlockSpec((tm, LANES), lambda i, j: (i, j)),
                  pl.BlockSpec((tm, LANES), lambda i, j: (i, zb + j)),
                  pl.BlockSpec((1, LANES), lambda i, j: (0, 0))],
        out_specs=pl.BlockSpec((tm, LANES), lambda i, j: (i, j)),
        out_shape=jax.ShapeDtypeStruct((m, H * DH), F32),
        compiler_params=_cparams(2, 32),
        name="onorm_gate",
    )(o, proj, gain.reshape(1, LANES))


def _gates_kernel(s_ref, bias_ref, alog_ref, g_ref, c_ref, carry, *, tm):
    @pl.when(pl.program_id(1) == 0)
    def _():
        carry[...] = jnp.zeros_like(carry)

    t = s_ref[0] + bias_ref[...]
    lane = lax.broadcasted_iota(jnp.int32, t.shape, 1)
    row = lax.broadcasted_iota(jnp.int32, t.shape, 0)
    l1p = _softplus_neg_abs(t)
    sig = jax.nn.sigmoid(t)
    g = -jnp.exp(alog_ref[...]) * (jnp.maximum(t, 0.0) + l1p)
    lsig = jnp.minimum(t, 0.0) - l1p
    gt = jnp.where(lane < G_L, sig, jnp.where(lane < LOGF_L, g, lsig))
    gt = jnp.where(lane < LOGF_L + H, gt, 0.0)
    g_ref[0] = gt
    cs = gt
    k = 1
    while k < tm:
        cs = cs + jnp.where(row >= k, pltpu.roll(cs, k, 0), 0.0)
        k *= 2
    cs = cs + carry[0:1, :]
    c_ref[0] = cs
    carry[...] = jnp.broadcast_to(cs[tm - 1:tm, :], carry.shape)


def even_gates(proj3, bias_vec, alog_vec, tm):
    b, t, _ = proj3.shape
    sb = SMALL // LANES
    return pl.pallas_call(
        functools.partial(_gates_kernel, tm=tm),
        grid=(b, t // tm),
        in_specs=[pl.BlockSpec((1, tm, LANES), lambda i, j: (i, j, sb)),
                  pl.BlockSpec((1, LANES), lambda i, j: (0, 0)),
                  pl.BlockSpec((1, LANES), lambda i, j: (0, 0))],
        out_specs=[pl.BlockSpec((1, tm, LANES), lambda i, j: (i, j, 0)),
                   pl.BlockSpec((1, tm, LANES), lambda i, j: (i, j, 0))],
        out_shape=[jax.ShapeDtypeStruct((b, t, LANES), F32),
                   jax.ShapeDtypeStruct((b, t, LANES), F32)],
        scratch_shapes=[pltpu.VMEM((SUBLANES, LANES), F32)],
        compiler_params=_cparams(2, 32),
        name="even_gates",
    )(proj3, bias_vec, alog_vec)


def _bmm(a, b, spec):
    return jnp.einsum(spec, a, b, precision=HI, preferred_element_type=F32)


def _delta_wy(q3, k3, v3, beta, g3):
    n, L, _ = q3.shape
    ii = lax.broadcasted_iota(jnp.int32, (n, L, L), 1)
    jj = lax.broadcasted_iota(jnp.int32, (n, L, L), 2)
    g_row = jnp.sum(jnp.where(ii == jj, jnp.broadcast_to(g3, (n, L, L)), 0.0), axis=1, keepdims=True)
    decay = jnp.where(ii >= jj, jnp.exp(jnp.where(ii >= jj, g3 - g_row, 0.0)), 0.0)
    kb3 = k3 * beta
    a = jnp.where(ii > jj, _bmm(kb3, k3, 'cid,cjd->cij') * decay, 0.0)
    pk = -a
    tinv = (ii == jj).astype(F32) + pk
    span = 2
    while span < L:
        pk = _bmm(pk, pk, 'cij,cjk->cik')
        tinv = tinv + _bmm(tinv, pk, 'cij,cjk->cik')
        span *= 2
    eg = jnp.exp(g3)
    u0 = _bmm(tinv, v3 * beta, 'cij,cjd->cid')
    wk = _bmm(tinv, kb3 * eg, 'cij,cjd->cid')
    qk = _bmm(q3, k3, 'cid,cjd->cij') * decay
    glast = g3[:, L - 1:L, :]
    return u0, wk, qk, k3 * jnp.exp(glast - g3), q3 * eg, jnp.exp(glast)


def _delta_heads_kernel(qkv_ref, gt_ref, s0_ref, o_ref, sn_ref):
    L = DELTA_CHUNK
    gt = gt_ref[0]
    row = lax.broadcasted_iota(jnp.int32, gt.shape, 0)
    cs = gt
    step = 1
    while step < L:
        cs = cs + jnp.where(row >= step, pltpu.roll(cs, step, 0), 0.0)
        step *= 2
    heads = lambda part: jnp.stack([qkv_ref[0, :, (part * H + h) * DH:(part * H + h + 1) * DH] for h in range(H)])
    beta = jnp.stack([gt[:, BETA_L + h:BETA_L + h + 1] for h in range(H)])
    g3 = jnp.stack([cs[:, G_L + h:G_L + h + 1] for h in range(H)])
    u0, wk, qk, kd, qg, egl = _delta_wy(heads(0), heads(1), heads(2), beta, g3)
    s0 = s0_ref[0]
    v_new = u0 - _bmm(wk, s0, 'hld,hdv->hlv')
    o = _bmm(qg, s0, 'hld,hdv->hlv') + _bmm(qk, v_new, 'hij,hjv->hiv')
    for h in range(H):
        o_ref[0, :, h * DH:(h + 1) * DH] = o[h]
        sn_ref[0, h] = s0[h] * egl[h] + lax.dot_general(kd[h], v_new[h], (((0,), (0,)), ((), ())),
                                                        precision=HI, preferred_element_type=F32)


def delta_rule_one_chunk(qkv3, gates, s0):
    b, t, _ = qkv3.shape
    assert t == DELTA_CHUNK
    return pl.pallas_call(
        _delta_heads_kernel,
        grid=(b,),
        in_specs=[pl.BlockSpec((1, t, A_CONV), lambda i: (i, 0, 0)),
                  pl.BlockSpec((1, t, LANES), lambda i: (i, 0, 0)),
                  pl.BlockSpec((1, H, DH, DH), lambda i: (i, 0, 0, 0))],
        out_specs=[pl.BlockSpec((1, t, H * DH), lambda i: (i, 0, 0)),
                   pl.BlockSpec((1, H, DH, DH), lambda i: (i, 0, 0, 0))],
        out_shape=[jax.ShapeDtypeStruct((b, t, H * DH), F32),
                   jax.ShapeDtypeStruct((b, H, DH, DH), F32)],
        compiler_params=_cparams(1, 32),
        name="delta_rule_one_chunk",
    )(qkv3, gates, s0)


def _delta_kernel(q_ref, k_ref, v_ref, gt_ref, s0_ref, o_ref, sn_ref,
                  wk_s, u0_s, qg_s, kd_s, qk_s, eg_s, *, t_len):
    L = DELTA_CHUNK
    nc = t_len // L
    h = pl.program_id(1)
    gt = gt_ref[0]
    lane = lax.broadcasted_iota(jnp.int32, gt.shape, 1)
    pos = lax.broadcasted_iota(jnp.int32, gt.shape, 0) % L
    onehot = (lane == G_L + h).astype(F32)
    cs = gt * onehot
    step = 1
    while step < L:
        cs = cs + jnp.where(pos >= step, pltpu.roll(cs, step, 0), 0.0)
        step *= 2
    gt3 = gt.reshape(nc, L, LANES)
    lane3 = lax.broadcasted_iota(jnp.int32, gt3.shape, 2)
    beta = jnp.sum(jnp.where(lane3 == BETA_L + h, gt3, 0.0), axis=-1, keepdims=True)
    g3 = jnp.sum(cs.reshape(nc, L, LANES), axis=-1, keepdims=True)
    u0, wk, qk, kd, qg, egl = _delta_wy(q_ref[0].reshape(nc, L, DH), k_ref[0].reshape(nc, L, DH),
                                        v_ref[0].reshape(nc, L, DH), beta, g3)
    u0_s[...] = u0
    wk_s[...] = wk
    qk_s[...] = qk
    kd_s[...] = kd
    qg_s[...] = qg
    eg_s[...] = jnp.broadcast_to(egl, (nc, 1, LANES))

    def body(c, s):
        v_new = u0_s[c] - jnp.dot(wk_s[c], s, precision=HI, preferred_element_type=F32)
        o = (jnp.dot(qg_s[c], s, precision=HI, preferred_element_type=F32)
             + jnp.dot(qk_s[c], v_new, precision=HI, preferred_element_type=F32))
        o_ref[0, pl.ds(pl.multiple_of(c * L, L), L), :] = o
        return s * eg_s[c] + lax.dot_general(kd_s[c], v_new, (((0,), (0,)), ((), ())),
                                             precision=HI, preferred_element_type=F32)

    sn_ref[0, 0] = lax.fori_loop(0, nc, body, s0_ref[0, 0])


def delta_rule(qkv3, gates, s0):
    b, t, _ = qkv3.shape
    nc = t // DELTA_CHUNK
    L = DELTA_CHUNK
    return pl.pallas_call(
        functools.partial(_delta_kernel, t_len=t),
        grid=(b, H),
        in_specs=[pl.BlockSpec((1, t, DH), lambda i, j: (i, 0, j)),
                  pl.BlockSpec((1, t, DH), lambda i, j: (i, 0, H + j)),
                  pl.BlockSpec((1, t, DH), lambda i, j: (i, 0, 2 * H + j)),
                  pl.BlockSpec((1, t, LANES), lambda i, j: (i, 0, 0)),
                  pl.BlockSpec((1, 1, DH, DH), lambda i, j: (i, j, 0, 0))],
        out_specs=[pl.BlockSpec((1, t, DH), lambda i, j: (i, 0, j)),
                   pl.BlockSpec((1, 1, DH, DH), lambda i, j: (i, j, 0, 0))],
        out_shape=[jax.ShapeDtypeStruct((b, t, H * DH), F32),
                   jax.ShapeDtypeStruct((b, H, DH, DH), F32)],
        scratch_shapes=[pltpu.VMEM((nc, L, DH), F32), pltpu.VMEM((nc, L, DH), F32),
                        pltpu.VMEM((nc, L, DH), F32), pltpu.VMEM((nc, L, DH), F32),
                        pltpu.VMEM((nc, L, L), F32), pltpu.VMEM((nc, 1, LANES), F32)],
        compiler_params=_cparams(2, 48),
        name="delta_rule",
    )(qkv3, qkv3, qkv3, gates, s0)


def _last_kblock(qi, tq, tk):
    return ((qi + 1) * tq - 1) // tk


def _online_softmax(s, v_bf, m_ref, l_ref, acc_ref):
    m_prev = m_ref[...]
    m_new = jnp.maximum(m_prev, jnp.max(s, axis=-1, keepdims=True))
    alpha = jnp.exp(m_prev - m_new)
    p = jnp.exp(s - m_new)
    l_ref[...] = alpha * l_ref[...] + jnp.sum(p, axis=-1, keepdims=True)
    acc_ref[...] = alpha * acc_ref[...] + jnp.dot(p.astype(BF16), v_bf, preferred_element_type=F32)
    m_ref[...] = m_new


def _fox_p_kernel(q_ref, k_ref, v_ref, cq_ref, ck_ref, o_ref, m_s, l_s, acc_s, *, tq, tk):
    h = pl.program_id(1)
    qi = pl.program_id(2)
    ki = pl.program_id(3)
    last = _last_kblock(qi, tq, tk)

    @pl.when(ki == 0)
    def _():
        m_s[...] = jnp.full(m_s.shape, -jnp.inf, F32)
        l_s[...] = jnp.zeros_like(l_s)
        acc_s[...] = jnp.zeros_like(acc_s)

    def step(masked):
        s = _nt(q_ref[0].astype(BF16), k_ref[0].astype(BF16)) * (DH ** -0.5)
        cqb = cq_ref[0]
        lane = lax.broadcasted_iota(jnp.int32, cqb.shape, 1)
        cq = jnp.sum(jnp.where(lane == LOGF_L + h, cqb, 0.0), axis=-1, keepdims=True)
        s = s + cq - ck_ref[0, 0]
        if masked:
            qpos = qi * tq + lax.broadcasted_iota(jnp.int32, s.shape, 0)
            kpos = ki * tk + lax.broadcasted_iota(jnp.int32, s.shape, 1)
            s = jnp.where(kpos <= qpos, s, -jnp.inf)
        _online_softmax(s, v_ref[0].astype(BF16), m_s, l_s, acc_s)

    straddles = (ki + 1) * tk - 1 > qi * tq

    @pl.when(jnp.logical_and(ki <= last, straddles))
    def _():
        step(True)

    @pl.when(jnp.logical_and(ki <= last, jnp.logical_not(straddles)))
    def _():
        step(False)

    @pl.when(ki == last)
    def _():
        o_ref[0] = acc_s[...] / l_s[...]


def fox_prompt(qn3, kn3, proj3, cum, cum_t, tq, tk):
    b, t, _ = qn3.shape
    vb = V_B // LANES
    kidx = lambda bi, h, qi, ki: jnp.minimum(ki, _last_kblock(qi, tq, tk))
    return pl.pallas_call(
        functools.partial(_fox_p_kernel, tq=tq, tk=tk),
        grid=(b, H, t // tq, t // tk),
        in_specs=[pl.BlockSpec((1, tq, DH), lambda bi, h, qi, ki: (bi, qi, h)),
                  pl.BlockSpec((1, tk, DH), lambda bi, h, qi, ki: (bi, kidx(bi, h, qi, ki), h)),
                  pl.BlockSpec((1, tk, DH), lambda bi, h, qi, ki: (bi, kidx(bi, h, qi, ki), vb + h)),
                  pl.BlockSpec((1, tq, LANES), lambda bi, h, qi, ki: (bi, qi, 0)),
                  pl.BlockSpec((1, 1, 1, tk), lambda bi, h, qi, ki: (bi, h, 0, kidx(bi, h, qi, ki)))],
        out_specs=pl.BlockSpec((1, tq, DH), lambda bi, h, qi, ki: (bi, qi, h)),
        out_shape=jax.ShapeDtypeStruct((b, t, H * DH), F32),
        scratch_shapes=[pltpu.VMEM((tq, 1), F32), pltpu.VMEM((tq, 1), F32), pltpu.VMEM((tq, DH), F32)],
        compiler_params=_cparams(4, 32),
        name="fox_prompt",
    )(qn3, kn3, proj3, cum, cum_t)


def _t5_tile_kernel(rb_ref, o_ref, *, tb):
    h = pl.program_id(0)
    off = pl.program_id(1)
    dist = (off * tb + lax.broadcasted_iota(jnp.int32, (tb, tb), 0)
            - lax.broadcasted_iota(jnp.int32, (tb, tb), 1))
    o_ref[0, 0] = _t5_bias(dist, rb_ref, h)


def t5_tiles(rel_bias, tb):
    return pl.pallas_call(
        functools.partial(_t5_tile_kernel, tb=tb),
        grid=(H, 2),
        in_specs=[pl.BlockSpec(memory_space=pltpu.SMEM)],
        out_specs=pl.BlockSpec((1, 1, tb, tb), lambda h, o: (h, o, 0, 0)),
        out_shape=jax.ShapeDtypeStruct((H, 2, tb, tb), F32),
        compiler_params=_cparams(2, 32),
        name="t5_tiles",
    )(rel_bias)


def _diff_lambda(lam_ref, lam_init):
    lp = lam_ref[...]
    s01 = jnp.sum(lp[0:1] * lp[1:2], axis=-1, keepdims=True)
    s23 = jnp.sum(lp[2:3] * lp[3:4], axis=-1, keepdims=True)
    return jnp.exp(s01) - jnp.exp(s23) + lam_init


def _diff_finish(acc0, l0, acc1, l1, lam, g, lam_init):
    o = acc0 / l0 - lam * (acc1 / l1)
    return o * lax.rsqrt(jnp.mean(o * o, axis=-1, keepdims=True) + EPS) * g * (1.0 - lam_init)


def _split_halves(qf):
    lane = lax.broadcasted_iota(jnp.int32, qf.shape, 1)
    return jnp.concatenate([jnp.where(lane < DQ_C, qf, 0.0), jnp.where(lane >= DQ_C, qf, 0.0)], axis=0)


def _diff_p_kernel(rb_ref, q_ref, k_ref, v_ref, bias_ref, lam_ref, g_ref, o_ref, m_s, l_s, acc_s,
                   *, tb, lam_init):
    h = pl.program_id(1)
    qi = pl.program_id(2)
    ki = pl.program_id(3)

    @pl.when(ki == 0)
    def _():
        m_s[...] = jnp.full(m_s.shape, -jnp.inf, F32)
        l_s[...] = jnp.zeros_like(l_s)
        acc_s[...] = jnp.zeros_like(acc_s)

    def step(on_diagonal):
        q2 = _split_halves(q_ref[0]).astype(BF16)
        near = (qi - ki) <= 1
        bias = jnp.where(near, bias_ref[0, 0], rb_ref[REL_BUCKETS - 1, h])
        s = _nt(q2, k_ref[0].astype(BF16)) * (DQ_C ** -0.5) + jnp.concatenate([bias, bias], axis=0)
        if on_diagonal:
            row = lax.broadcasted_iota(jnp.int32, s.shape, 0)
            qpos = jnp.where(row >= tb, row - tb, row)
            s = jnp.where(lax.broadcasted_iota(jnp.int32, s.shape, 1) <= qpos, s, -jnp.inf)
        _online_softmax(s, v_ref[0].astype(BF16), m_s, l_s, acc_s)

    @pl.when(ki < qi)
    def _():
        step(False)

    @pl.when(ki == qi)
    def _():
        step(True)
        lam = _diff_lambda(lam_ref, lam_init)
        o_ref[0] = _diff_finish(acc_s[0:tb], l_s[0:tb], acc_s[tb:2 * tb], l_s[tb:2 * tb], lam, g_ref[...], lam_init)


def diff_prompt(qn3, kn3, proj3, v_col0, tiles, rel_bias, lam_p, subln_g, lam_init, tb):
    b, t, _ = qn3.shape
    vb = v_col0 // LANES
    kidx = lambda qi, ki: jnp.minimum(ki, qi)
    return pl.pallas_call(
        functools.partial(_diff_p_kernel, tb=tb, lam_init=lam_init),
        grid=(b, H, t // tb, t // tb),
        in_specs=[pl.BlockSpec(memory_space=pltpu.SMEM),
                  pl.BlockSpec((1, tb, DH), lambda bi, h, qi, ki: (bi, qi, h)),
                  pl.BlockSpec((1, tb, DH), lambda bi, h, qi, ki: (bi, kidx(qi, ki), h)),
                  pl.BlockSpec((1, tb, DH), lambda bi, h, qi, ki: (bi, kidx(qi, ki), vb + h)),
                  pl.BlockSpec((1, 1, tb, tb), lambda bi, h, qi, ki: (h, jnp.clip(qi - ki, 0, 1), 0, 0)),
                  pl.BlockSpec((4, DQ_C), lambda bi, h, qi, ki: (0, 0)),
                  pl.BlockSpec((1, DH), lambda bi, h, qi, ki: (0, 0))],
        out_specs=pl.BlockSpec((1, tb, DH), lambda bi, h, qi, ki: (bi, qi, h)),
        out_shape=jax.ShapeDtypeStruct((b, t, H * DH), F32),
        scratch_shapes=[pltpu.VMEM((2 * tb, 1), F32), pltpu.VMEM((2 * tb, 1), F32), pltpu.VMEM((2 * tb, DH), F32)],
        compiler_params=_cparams(4, 40),
        name="diff_prompt",
    )(rel_bias, qn3, kn3, proj3, tiles, lam_p, subln_g.reshape(1, DH))


def _suffix_matrix(n):
    return (lax.broadcasted_iota(jnp.int32, (n, n), 0) > lax.broadcasted_iota(jnp.int32, (n, n), 1)).astype(BF16)


def _rows_times_01(x, u, parts):
    m = x.shape[0]
    terms = []
    rest = x
    for _ in range(parts):
        term = rest.astype(BF16).astype(F32)
        terms.append(term)
        rest = rest - term
    y = jnp.dot(jnp.concatenate(terms, axis=0).astype(BF16), u, preferred_element_type=F32)
    out = y[0:m]
    for i in range(1, parts):
        out = out + y[i * m:(i + 1) * m]
    return out


SB_UNDERFLOW = -104.0


def _sb_logs(z):
    l1p = _softplus_neg_abs(z)
    return jnp.minimum(z, 0.0) - l1p, jnp.minimum(-z, 0.0) - l1p


def _sb_p_kernel(q_ref, k_ref, v_ref, o_ref, carry_s, acc_s, *, tq, tk):
    qi = pl.program_id(2)
    ki = pl.program_id(3)
    last = _last_kblock(qi, tq, tk)
    kb = last - ki

    @pl.when(ki == 0)
    def _():
        carry_s[...] = jnp.zeros_like(carry_s)
        acc_s[...] = jnp.zeros_like(acc_s)

    def step(masked):
        z = _nt(q_ref[0].astype(BF16), k_ref[0].astype(BF16)) * (DH ** -0.5)
        log_sig, log_keep = _sb_logs(z)
        if masked:
            qpos = qi * tq + lax.broadcasted_iota(jnp.int32, z.shape, 0)
            kpos = kb * tk + lax.broadcasted_iota(jnp.int32, z.shape, 1)
            mask = kpos < qpos
            log_keep = jnp.where(mask, log_keep, 0.0)
        later = carry_s[...] + _rows_times_01(log_keep, _suffix_matrix(tk), 2)
        a = jnp.exp(log_sig + later)
        if masked:
            a = jnp.where(mask, a, 0.0)
        acc_s[...] += jnp.dot(a.astype(BF16), v_ref[0].astype(BF16), preferred_element_type=F32)
        carry_s[...] += jnp.sum(log_keep, axis=-1, keepdims=True)

    live = jnp.logical_and(ki <= last, jnp.max(carry_s[...]) > SB_UNDERFLOW)
    straddles = (kb + 1) * tk > qi * tq

    @pl.when(jnp.logical_and(live, straddles))
    def _():
        step(True)

    @pl.when(jnp.logical_and(live, jnp.logical_not(straddles)))
    def _():
        step(False)

    @pl.when(ki == last)
    def _():
        o_ref[0] = acc_s[...]


def sb_prompt(proj3, q_col0, k_col0, v_col0, tq, tk):
    b, t, _ = proj3.shape
    qb, kb, vb = q_col0 // LANES, k_col0 // LANES, v_col0 // LANES
    kidx = lambda qi, ki: jnp.maximum(_last_kblock(qi, tq, tk) - ki, 0)
    return pl.pallas_call(
        functools.partial(_sb_p_kernel, tq=tq, tk=tk),
        grid=(b, H, t // tq, t // tk),
        in_specs=[pl.BlockSpec((1, tq, DH), lambda bi, h, qi, ki: (bi, qi, qb + h)),
                  pl.BlockSpec((1, tk, DH), lambda bi, h, qi, ki: (bi, kidx(qi, ki), kb + h)),
                  pl.BlockSpec((1, tk, DH), lambda bi, h, qi, ki: (bi, kidx(qi, ki), vb + h))],
        out_specs=pl.BlockSpec((1, tq, DH), lambda bi, h, qi, ki: (bi, qi, h)),
        out_shape=jax.ShapeDtypeStruct((b, t, H * DH), F32),
        scratch_shapes=[pltpu.VMEM((tq, 1), F32), pltpu.VMEM((tq, DH), F32)],
        compiler_params=_cparams(4, 32),
        name="sb_prompt",
    )(proj3, proj3, proj3)


PAGES_PER_STEP = 4


def _pages_per_step(n_pages):
    return max(d for d in range(1, PAGES_PER_STEP + 1) if n_pages % d == 0)


def _slot_page(pt_ref, b, p, slot, n_pages, pps, layer_off):
    return layer_off + pt_ref[b, n_pages - 1 - (jnp.maximum(p - 1, 0) * pps + slot)]


def _head_rows_bf16(refs, h):
    parts = [r[0, :, h, :].astype(BF16) for r in refs]
    return parts[0] if len(parts) == 1 else jnp.concatenate(parts, axis=0)


def _qk_heads(q_list, k_refs):
    return jnp.concatenate([_nt(q_list[h], _head_rows_bf16(k_refs, h)) for h in range(H)], axis=0)


def _pv_heads(p, v_refs, rows):
    return jnp.concatenate(
        [jnp.dot(p[h * rows:(h + 1) * rows].astype(BF16), _head_rows_bf16(v_refs, h), preferred_element_type=F32)
         for h in range(H)], axis=0)


def _softmax_step(s, v_refs, rows, m_s, l_s, acc_s):
    m_prev = m_s[...]
    m_new = jnp.maximum(m_prev, jnp.max(s, axis=-1, keepdims=True))
    alpha = jnp.exp(m_prev - m_new)
    p = jnp.exp(s - m_new)
    l_s[...] = alpha * l_s[...] + jnp.sum(p, axis=-1, keepdims=True)
    acc_s[...] = alpha * acc_s[...] + _pv_heads(p, v_refs, rows)
    m_s[...] = m_new


def _row_in_group(shape, group):
    return lax.broadcasted_iota(jnp.int32, shape, 0) % group


def _flat_pages_bf16(refs):
    parts = [r[0].reshape(PAGE * H, DH).astype(BF16) for r in refs]
    return parts[0] if len(parts) == 1 else jnp.concatenate(parts, axis=0)


def _flat_softmax_step(s, v_refs, m_s, l_s, acc_s):
    m_prev = m_s[...]
    m_new = jnp.maximum(m_prev, jnp.max(s, axis=-1, keepdims=True))
    alpha = jnp.exp(m_prev - m_new)
    p = jnp.exp(s - m_new)
    l_s[...] = alpha * l_s[...] + jnp.sum(p, axis=-1, keepdims=True)
    acc_s[...] = alpha * acc_s[...] + jnp.dot(p.astype(BF16), _flat_pages_bf16(v_refs), preferred_element_type=F32)
    m_s[...] = m_new


def _head_match(shape, rows_per_head):
    col_head = lax.broadcasted_iota(jnp.int32, shape, 1) % H
    return col_head == lax.broadcasted_iota(jnp.int32, shape, 0) // rows_per_head


def _causal_new(shape, t_len):
    return lax.broadcasted_iota(jnp.int32, shape, 1) // H <= _row_in_group(shape, t_len)


def _fox_s_kernel(pt_ref, q_ref, kn_ref, vn_ref, cumflat_ref, cumn_ref, *rest, n_pages, pps, t_len):
    kp_refs, vp_refs = rest[:pps], rest[pps:2 * pps]
    suf_refs, tot_refs = rest[2 * pps:3 * pps], rest[3 * pps:4 * pps]
    o_ref, m_s, l_s, acc_s, carry_s = rest[4 * pps:]
    p = pl.program_id(1)
    q_all = jnp.concatenate([q_ref[0, :, h * DH:(h + 1) * DH] for h in range(H)], axis=0).astype(BF16)

    def logits(k_refs, key_bias):
        s = _nt(q_all, _flat_pages_bf16(k_refs)) * (DH ** -0.5) + key_bias + cumn_ref[0]
        return s, _head_match(s.shape, t_len)

    @pl.when(p == 0)
    def _():
        m_s[...] = jnp.full(m_s.shape, -jnp.inf, F32)
        l_s[...] = jnp.zeros_like(l_s)
        acc_s[...] = jnp.zeros_like(acc_s)
        carry_s[...] = jnp.zeros_like(carry_s)
        s, ok = logits([kn_ref], -cumflat_ref[0])
        ok = jnp.logical_and(ok, _causal_new(s.shape, t_len))
        _flat_softmax_step(jnp.where(ok, s, -jnp.inf), [vn_ref], m_s, l_s, acc_s)

    @pl.when(p > 0)
    def _():
        off = carry_s[...]
        cols = []
        for i in range(pps):
            cols.append(suf_refs[i][0] + off)
            off = off + tot_refs[i][0]
        s, ok = logits(kp_refs, cols[0] if pps == 1 else jnp.concatenate(cols, axis=1))
        _flat_softmax_step(jnp.where(ok, s, -jnp.inf), vp_refs, m_s, l_s, acc_s)
        carry_s[...] = off

    @pl.when(p == n_pages // pps)
    def _():
        o = acc_s[...] / l_s[...]
        for h in range(H):
            o_ref[0, :, h * DH:(h + 1) * DH] = o[h * t_len:(h + 1) * t_len]


def _page_suffix_kernel(x_ref, suf_ref, tot_ref):
    x = x_ref[...]
    suf_ref[...] = _rows_times_01(x, _suffix_matrix(x.shape[1]), 3)
    tot_ref[...] = jnp.broadcast_to(jnp.sum(x, axis=-1, keepdims=True), x.shape)


def page_suffix(x, tm):
    r, n = x.shape
    return pl.pallas_call(
        _page_suffix_kernel,
        grid=(r // tm,),
        in_specs=[pl.BlockSpec((tm, n), lambda i: (i, 0))],
        out_specs=[pl.BlockSpec((tm, n), lambda i: (i, 0)), pl.BlockSpec((tm, n), lambda i: (i, 0))],
        out_shape=[jax.ShapeDtypeStruct((r, n), F32), jax.ShapeDtypeStruct((r, n), F32)],
        compiler_params=_cparams(1, 32),
        name="page_suffix",
    )(x)


def _pool_specs(block, n_pages, pps, layer_off):
    def idx(slot):
        return lambda b, p, pt: (_slot_page(pt, b, p, slot, n_pages, pps, layer_off),) + (0,) * (len(block) - 1)
    return [pl.BlockSpec(block, idx(slot)) for slot in range(pps)]


def _qkv_specs(t_len, n_pages, pps, layer_off):
    page_block = (1, PAGE, H, DH)
    specs = [pl.BlockSpec((1, t_len, H * DH), lambda b, p, pt: (b, 0, 0)),
             pl.BlockSpec(page_block, lambda b, p, pt: (b, 0, 0, 0)),
             pl.BlockSpec(page_block, lambda b, p, pt: (b, 0, 0, 0))]
    return (specs + _pool_specs(page_block, n_pages, pps, layer_off)
            + _pool_specs(page_block, n_pages, pps, layer_off))


def fox_sample(pt, qn3, kn4, vn4, cumn_flat, cumn_col, kpool, vpool, suf_flat, tot_flat, layer_off):
    b, t, row = qn3.shape
    n_pages = pt.shape[1]
    pps = _pages_per_step(n_pages)
    flat = PAGE * H
    specs = _qkv_specs(t, n_pages, pps, layer_off)
    in_specs = specs[:3] + [pl.BlockSpec((1, 1, flat), lambda bi, p, ptr: (bi, 0, 0)),
                            pl.BlockSpec((1, H * t, 1), lambda bi, p, ptr: (bi, 0, 0))]
    in_specs += specs[3:] + _pool_specs((1, 1, flat), n_pages, pps, layer_off)
    in_specs += _pool_specs((1, 1, flat), n_pages, pps, layer_off)
    return pl.pallas_call(
        functools.partial(_fox_s_kernel, n_pages=n_pages, pps=pps, t_len=t),
        grid_spec=pltpu.PrefetchScalarGridSpec(
            num_scalar_prefetch=1,
            grid=(b, n_pages // pps + 1),
            in_specs=in_specs,
            out_specs=pl.BlockSpec((1, t, row), lambda bi, p, ptr: (bi, 0, 0)),
            scratch_shapes=[pltpu.VMEM((H * t, 1), F32), pltpu.VMEM((H * t, 1), F32),
                            pltpu.VMEM((H * t, DH), F32), pltpu.VMEM((1, flat), F32)]),
        out_shape=jax.ShapeDtypeStruct((b, t, row), F32),
        compiler_params=_cparams(2, 40),
        name="fox_sample",
    )(pt, qn3, kn4, vn4, cumn_flat, cumn_col, *([kpool] * pps), *([vpool] * pps),
      *([suf_flat] * pps), *([tot_flat] * pps))


def _diff_s_kernel(pt_ref, rbrows_ref, q_ref, kn_ref, vn_ref, *rest, n_pages, pps, t_len, lam_init):
    kp_refs, vp_refs = rest[:pps], rest[pps:2 * pps]
    lam_ref, g_ref, o_ref, m_s, l_s, acc_s = rest[2 * pps:]
    p = pl.program_id(1)
    rows = 2 * t_len
    n_rows = H * rows
    flat = PAGE * H
    q_all = jnp.concatenate([_split_halves(q_ref[0, :, h * DH:(h + 1) * DH]) for h in range(H)],
                            axis=0).astype(BF16)
    rb = rbrows_ref[...]
    far = rb[:, REL_BUCKETS - 1:REL_BUCKETS]

    def near_bias(offset):
        shape = (n_rows, flat)
        dist = offset + _row_in_group(shape, t_len) - lax.broadcasted_iota(jnp.int32, shape, 1) // H
        val = jnp.broadcast_to(far, shape)
        for b in range(REL_BUCKETS - 2, -1, -1):
            val = jnp.where(dist < T5_STARTS[b + 1], rb[:, b:b + 1], val)
        return val

    def process(k_refs, v_refs, mode):
        n = len(k_refs)
        s = _nt(q_all, _flat_pages_bf16(k_refs)) * (DQ_C ** -0.5)
        ok = _head_match(s.shape, rows)
        if mode == "far":
            s = s + far
        else:
            bias = near_bias(0 if mode == "new" else PAGE)
            if n > 1:
                bias = jnp.concatenate([bias, jnp.broadcast_to(far, (n_rows, (n - 1) * flat))], axis=1)
            s = s + bias
        if mode == "new":
            ok = jnp.logical_and(ok, _causal_new(s.shape, t_len))
        _flat_softmax_step(jnp.where(ok, s, -jnp.inf), v_refs, m_s, l_s, acc_s)

    @pl.when(p == 0)
    def _():
        m_s[...] = jnp.full(m_s.shape, -jnp.inf, F32)
        l_s[...] = jnp.zeros_like(l_s)
        acc_s[...] = jnp.zeros_like(acc_s)
        process([kn_ref], [vn_ref], "new")

    @pl.when(p == 1)
    def _():
        process(kp_refs, vp_refs, "last")

    @pl.when(p > 1)
    def _():
        process(kp_refs, vp_refs, "far")

    @pl.when(p == n_pages // pps)
    def _():
        lam = _diff_lambda(lam_ref, lam_init)
        acc = acc_s[...]
        l = l_s[...]
        for h in range(H):
            r0, r1, r2 = h * rows, h * rows + t_len, (h + 1) * rows
            o_ref[0, :, h * DH:(h + 1) * DH] = _diff_finish(acc[r0:r1], l[r0:r1], acc[r1:r2], l[r1:r2],
                                                            lam, g_ref[...], lam_init)


def diff_sample(pt, rel_bias, qn3, kn4, vn4, kpool, vpool, lam_p, subln_g, lam_init, layer_off):
    b, t, row = qn3.shape
    n_pages = pt.shape[1]
    pps = _pages_per_step(n_pages)
    rb_rows = jnp.repeat(rel_bias.T, 2 * t, axis=0)
    in_specs = [pl.BlockSpec((2 * H * t, REL_BUCKETS), lambda bi, p, ptr: (0, 0))]
    in_specs += _qkv_specs(t, n_pages, pps, layer_off)
    in_specs += [pl.BlockSpec((4, DQ_C), lambda bi, p, ptr: (0, 0)),
                 pl.BlockSpec((1, DH), lambda bi, p, ptr: (0, 0))]
    return pl.pallas_call(
        functools.partial(_diff_s_kernel, n_pages=n_pages, pps=pps, t_len=t, lam_init=lam_init),
        grid_spec=pltpu.PrefetchScalarGridSpec(
            num_scalar_prefetch=1,
            grid=(b, n_pages // pps + 1),
            in_specs=in_specs,
            out_specs=pl.BlockSpec((1, t, row), lambda bi, p, ptr: (bi, 0, 0)),
            scratch_shapes=[pltpu.VMEM((2 * H * t, 1), F32), pltpu.VMEM((2 * H * t, 1), F32),
                            pltpu.VMEM((2 * H * t, DH), F32)]),
        out_shape=jax.ShapeDtypeStruct((b, t, row), F32),
        compiler_params=_cparams(2, 40),
        name="diff_sample",
    )(pt, rb_rows, qn3, kn4, vn4, *([kpool] * pps), *([vpool] * pps), lam_p, subln_g.reshape(1, DH))


def _sb_s_kernel(pt_ref, q_ref, kn_ref, vn_ref, *rest, n_pages, pps, t_len):
    kp_refs, vp_refs = rest[:pps], rest[pps:2 * pps]
    o_ref, carry_s, acc_s = rest[2 * pps:]
    p = pl.program_id(1)
    u = _suffix_matrix(PAGE)
    q_list = [q_ref[0, :, h * DH:(h + 1) * DH].astype(BF16) for h in range(H)]
    rows = H * t_len

    def process(k_refs, v_refs, is_new):
        n = len(k_refs)
        z = _qk_heads(q_list, k_refs) * (DH ** -0.5)
        log_sig, log_keep = _sb_logs(z)
        if is_new:
            mask = lax.broadcasted_iota(jnp.int32, z.shape, 1) < _row_in_group(z.shape, t_len)
            log_keep = jnp.where(mask, log_keep, 0.0)
        pages = [log_keep[:, i * PAGE:(i + 1) * PAGE] for i in range(n)]
        suf = _rows_times_01(pages[0] if n == 1 else jnp.concatenate(pages, axis=0), u, 2)
        off = carry_s[...]
        cols = []
        for i in range(n):
            cols.append(suf[i * rows:(i + 1) * rows] + off)
            off = off + jnp.sum(pages[i], axis=-1, keepdims=True)
        a = jnp.exp(log_sig + (cols[0] if n == 1 else jnp.concatenate(cols, axis=1)))
        if is_new:
            a = jnp.where(mask, a, 0.0)
        acc_s[...] += _pv_heads(a, v_refs, t_len)
        carry_s[...] = off

    @pl.when(p == 0)
    def _():
        carry_s[...] = jnp.zeros_like(carry_s)
        acc_s[...] = jnp.zeros_like(acc_s)
        process([kn_ref], [vn_ref], True)

    @pl.when(jnp.logical_and(p > 0, jnp.max(carry_s[...]) > SB_UNDERFLOW))
    def _():
        process(kp_refs, vp_refs, False)

    @pl.when(p == n_pages // pps)
    def _():
        acc = acc_s[...]
        for h in range(H):
            o_ref[0, :, h * DH:(h + 1) * DH] = acc[h * t_len:(h + 1) * t_len]


def sb_sample(pt, q3, kn4, vn4, kpool, vpool, layer_off):
    b, t, row = q3.shape
    n_pages = pt.shape[1]
    pps = _pages_per_step(n_pages)
    return pl.pallas_call(
        functools.partial(_sb_s_kernel, n_pages=n_pages, pps=pps, t_len=t),
        grid_spec=pltpu.PrefetchScalarGridSpec(
            num_scalar_prefetch=1,
            grid=(b, n_pages // pps + 1),
            in_specs=_qkv_specs(t, n_pages, pps, layer_off),
            out_specs=pl.BlockSpec((1, t, row), lambda bi, p, ptr: (bi, 0, 0)),
            scratch_shapes=[pltpu.VMEM((H * t, 1), F32), pltpu.VMEM((H * t, DH), F32)]),
        out_shape=jax.ShapeDtypeStruct((b, t, row), F32),
        compiler_params=_cparams(2, 40),
        name="sb_sample",
    )(pt, q3, kn4, vn4, *([kpool] * pps), *([vpool] * pps))


def _pad_rows(a, rows):
    return jnp.pad(a, ((0, 0), (0, rows - a.shape[1]), (0, 0)))


def _new_page(rows3):
    return _pad_rows(rows3, PAGE).reshape(rows3.shape[0], PAGE, H, DH)


def _pool_pages(cache):
    return cache.reshape((-1,) + cache.shape[2:])


def _mod_arrays(mod, idx, t, per_token):
    m = mod[:, idx]
    if per_token:
        return jnp.repeat(m, t, axis=0)[None]
    return m[:, None, :]


ROW_TILE = 512
COL_TILE = 512
GATE_TILE = 512
FOX_TILES = (512, 512)
DIFF_TILE = 512
SB_TILES = (512, 256)


def _tiles(b, t, per_token):
    return min(ROW_TILE, b * t if per_token else t), COL_TILE


def _even_layer(x2, b, t, mod, wts, i, past):
    per_token = past is not None
    tm, tn = _tiles(b, t, per_token)
    proj = norm_mod_matmul(x2, wts['norm_g'][2 * i, 0], _mod_arrays(mod, 1, t, per_token),
                           _mod_arrays(mod, 0, t, per_token), wts['w_in_even'][i], t, tm, tn)
    proj3 = proj.reshape(b, t, EVEN_N)
    if past is None:
        conv_buf = jnp.zeros((b, SC_W - 1, A_CONV), F32)
        s0 = jnp.zeros((b, H, DH, DH), F32)
    else:
        conv_buf = past['state_conv_delta'][i]
        s0 = past['state_delta'][i]
    qkv3 = delta_conv(proj3, conv_buf, wts['delta_conv_w'][i], 1 if t > SUBLANES else b)
    gates, cum = even_gates(proj3, wts['gate_bias'][i], wts['gate_alog'][i], min(t, GATE_TILE))
    t_pad = -(-t // DELTA_CHUNK) * DELTA_CHUNK
    delta_fn = delta_rule_one_chunk if t_pad == DELTA_CHUNK else delta_rule
    o_a, s_new = delta_fn(_pad_rows(qkv3, t_pad), _pad_rows(gates, t_pad), s0)
    o_a = onorm_gate(o_a[:, :t].reshape(b * t, H * DH), proj, wts['delta_onorm_g'][i], tm)
    qn = head_norm(proj, Q_B, H * DH, wts['fox_qn_g'][i], tm)
    kn = head_norm(proj, K_B, H * DH, wts['fox_kn_g'][i], tm)
    qn3 = qn.reshape(b, t, H * DH)
    kn3 = kn.reshape(b, t, H * DH)
    v3 = proj3[:, :, V_B:V_B + H * DH]
    if past is None:
        cum_t = jnp.swapaxes(cum[:, :, LOGF_L:LOGF_L + H], 1, 2).reshape(b, H, 1, t)
        o_b = fox_prompt(qn3, kn3, proj3, cum, cum_t, min(t, FOX_TILES[0]), min(t, FOX_TILES[1]))
    else:
        pt = past['page_table']
        n_pool = past['cache_k_fox'].shape[1]
        cum_new = cum[:, :, LOGF_L:LOGF_L + H]
        cumn_flat = _pad_rows(cum_new, PAGE).reshape(b, 1, PAGE * H)
        cumn_col = jnp.swapaxes(cum_new, 1, 2).reshape(b, H * t, 1)
        suf_flat, tot_flat = past['fox_page_sums']
        o_b = fox_sample(pt, qn3, _new_page(kn3), _new_page(v3), cumn_flat, cumn_col,
                         _pool_pages(past['cache_k_fox']), _pool_pages(past['cache_v_fox']),
                         suf_flat, tot_flat, i * n_pool)
    m_out = (o_a, o_b.reshape(b * t, H * DH))
    rows = (kn3.reshape(b, t, H, DH), v3.reshape(b, t, H, DH), gates[:, :, LOGF_L:LOGF_L + H], s_new,
            proj3[:, t - (SC_W - 1):, :A_CONV])
    return m_out, rows


def _odd_layer(x2, b, t, mod, wts, i, l, past):
    per_token = past is not None
    tm, tn = _tiles(b, t, per_token)
    row = H * DH
    proj = norm_mod_matmul(x2, wts['norm_g'][l, 0], _mod_arrays(mod, 1, t, per_token),
                           _mod_arrays(mod, 0, t, per_token), wts['w_in_odd'][i], t, tm, tn)
    proj3 = proj.reshape(b, t, 6 * row)
    lam_init = 0.8 - 0.6 * math.exp(-0.3 * l)
    qcn3 = head_norm(proj, 0, row, wts['diff_qn_g'][i], tm).reshape(b, t, row)
    kcn3 = head_norm(proj, row, row, wts['diff_kn_g'][i], tm).reshape(b, t, row)
    vc3 = proj3[:, :, 2 * row:3 * row]
    kd3 = proj3[:, :, 4 * row:5 * row]
    vd3 = proj3[:, :, 5 * row:6 * row]
    if past is None:
        tb = min(t, DIFF_TILE)
        tiles = t5_tiles(wts['rel_bias'], tb)
        o_c = diff_prompt(qcn3, kcn3, proj3, 2 * row, tiles, wts['rel_bias'], wts['diff_lambda'][i],
                          wts['diff_subln_g'][i], lam_init, tb)
        o_d = sb_prompt(proj3, 3 * row, 4 * row, 5 * row, min(t, SB_TILES[0]), min(t, SB_TILES[1]))
    else:
        pt = past['page_table']
        n_pool = past['cache_k_diff'].shape[1]
        o_c = diff_sample(pt, wts['rel_bias'], qcn3, _new_page(kcn3), _new_page(vc3),
                          _pool_pages(past['cache_k_diff']), _pool_pages(past['cache_v_diff']),
                          wts['diff_lambda'][i], wts['diff_subln_g'][i], lam_init, i * n_pool)
        o_d = sb_sample(pt, proj3[:, :, 3 * row:4 * row], _new_page(kd3), _new_page(vd3),
                        _pool_pages(past['cache_k_sb']), _pool_pages(past['cache_v_sb']), i * n_pool)
    m_out = (o_c.reshape(b * t, row), o_d.reshape(b * t, row))
    rows = (kcn3.reshape(b, t, H, DH), vc3.reshape(b, t, H, DH), kd3.reshape(b, t, H, DH),
            vd3.reshape(b, t, H, DH))
    return m_out, rows


def _trunk(x, mods, wts, past):
    b, t, d = x.shape
    depth = wts['ffn_w_up'].shape[0]
    x2 = x.reshape(b * t, d)
    per_token = past is not None
    tm, tn = _tiles(b, t, per_token)
    even_rows, odd_rows, ffn_rows = [], [], []
    for l in range(depth):
        i = l // 2
        mod = mods[l]
        if l % 2 == 0:
            (a1, a2), rows = _even_layer(x2, b, t, mod, wts, i, past)
            even_rows.append(rows)
            w_out = wts['w_out_even'][i]
        else:
            (a1, a2), rows = _odd_layer(x2, b, t, mod, wts, i, l, past)
            odd_rows.append(rows)
            w_out = wts['w_out_odd'][i]
        x2 = matmul_gated_residual([a1, a2], w_out, x2, _mod_arrays(mod, 2, t, per_token), t, tm, tn)
        u = norm_mod_matmul(x2, wts['norm_g'][l, 1], _mod_arrays(mod, 4, t, per_token),
                            _mod_arrays(mod, 3, t, per_token), wts['ffn_w_up'][l], t, tm, tn)
        u3 = u.reshape(b, t, -1)
        if past is None:
            ffn_buf = jnp.zeros((b, FFN_CONV_W - 1, u3.shape[2]), F32)
        else:
            ffn_buf = past['state_conv_ffn'][l]
        act = ffn_act(u3, ffn_buf, wts['ffn_conv_w'][l], wts['ffn_conv_b'][l], 1 if t > SUBLANES else b, 512)
        ffn_rows.append(u3[:, t - (FFN_CONV_W - 1):])
        x2 = matmul_gated_residual([act.reshape(b * t, -1)], wts['ffn_w_down'][l], x2,
                                   _mod_arrays(mod, 5, t, per_token), t, tm, tn)
    stack = lambda rows_list, k: jnp.stack([r[k] for r in rows_list])
    state = [stack(even_rows, k) for k in range(5)] + [stack(odd_rows, k) for k in range(4)] + [jnp.stack(ffn_rows)]
    return x2.reshape(b, t, d), state


def _permute_even_in(w):
    row = H * DH
    c_small = A_CONV + row
    c_qb = c_small + 2 * H
    c_fb = c_qb + 3 * row
    parts = [w[..., :c_small], w[..., c_qb:c_fb], w[..., c_small:c_qb], w[..., c_fb:c_fb + H]]
    pad = EVEN_N - sum(p.shape[-1] for p in parts)
    return jnp.pad(jnp.concatenate(parts, axis=-1), ((0, 0), (0, 0), (0, pad)))


def kernel(x_prompt, x_sample, cache_k_fox, cache_v_fox, cache_logf_fox, cache_k_diff, cache_v_diff, cache_k_sb, cache_v_sb, state_delta, state_conv_delta, state_conv_ffn, page_table, c_prompt, c_sample, w_ada, b_ada, norm_g, w_in_even, w_out_even, delta_conv_w, delta_a_log, delta_dt_bias, delta_onorm_g, fox_qn_g, fox_kn_g, fox_f_bias, w_in_odd, w_out_odd, diff_qn_g, diff_kn_g, diff_lambda, diff_subln_g, rel_bias, ffn_w_up, ffn_conv_w, ffn_conv_b, ffn_w_down):
    bp, d = c_prompt.shape
    bs = c_sample.shape[0]
    n_even = w_in_even.shape[0]
    zeros8 = jnp.zeros((n_even, H), F32)
    lane_pad = jnp.zeros((n_even, LANES - 3 * H), F32)
    wts = {
        'norm_g': norm_g,
        'w_in_even': _permute_even_in(w_in_even).astype(BF16),
        'w_out_even': w_out_even.astype(BF16),
        'delta_conv_w': delta_conv_w,
        'gate_bias': jnp.concatenate([zeros8, delta_dt_bias, fox_f_bias, lane_pad], axis=-1)[:, None, :],
        'gate_alog': jnp.concatenate([zeros8, delta_a_log, zeros8, lane_pad], axis=-1)[:, None, :],
        'delta_onorm_g': delta_onorm_g, 'fox_qn_g': fox_qn_g, 'fox_kn_g': fox_kn_g,
        'w_in_odd': w_in_odd.astype(BF16), 'w_out_odd': w_out_odd.astype(BF16),
        'diff_qn_g': diff_qn_g, 'diff_kn_g': diff_kn_g, 'diff_lambda': diff_lambda,
        'diff_subln_g': diff_subln_g, 'rel_bias': rel_bias,
        'ffn_w_up': ffn_w_up.astype(BF16), 'ffn_conv_w': ffn_conv_w, 'ffn_conv_b': ffn_conv_b,
        'ffn_w_down': ffn_w_down.astype(BF16),
    }
    logf_rows = jnp.swapaxes(cache_logf_fox, 2, 3).reshape(-1, PAGE)
    page_sums = page_suffix(logf_rows, math.gcd(logf_rows.shape[0], 2048))
    to_cols = lambda a: jnp.swapaxes(a.reshape(-1, H, PAGE), 1, 2).reshape(-1, 1, PAGE * H)
    past = {
        'cache_k_fox': cache_k_fox, 'cache_v_fox': cache_v_fox,
        'fox_page_sums': (to_cols(page_sums[0]), to_cols(page_sums[1])),
        'cache_k_diff': cache_k_diff, 'cache_v_diff': cache_v_diff, 'cache_k_sb': cache_k_sb,
        'cache_v_sb': cache_v_sb, 'state_delta': state_delta, 'state_conv_delta': state_conv_delta,
        'state_conv_ffn': state_conv_ffn, 'page_table': page_table,
    }
    n_c = bp + bs
    c_all = _pad_rows(jnp.concatenate([c_prompt, c_sample], axis=0)[None], -(-n_c // SUBLANES) * SUBLANES)[0]
    mod_all = ada_mod(c_all, w_ada, b_ada)
    mods_p = [mod_all[l, :bp].reshape(bp, 6, d) for l in range(mod_all.shape[0])]
    mods_s = [mod_all[l, bp:n_c].reshape(bs, 6, d) for l in range(mod_all.shape[0])]
    y_prompt, sp = _trunk(x_prompt, mods_p, wts, None)
    y_sample, ss = _trunk(x_sample, mods_s, wts, past)
    return (y_prompt, y_sample, *sp, *ss)
```

```python
import functools
import math

import numpy as np
import jax
import jax.numpy as jnp
from jax import lax
from jax.experimental import pallas as pl
from jax.experimental.pallas import tpu as pltpu

F32 = jnp.float32
BF16 = jnp.bfloat16
HI = lax.Precision.HIGHEST

LANES = 128
SUBLANES = 8
MIB = 1024 * 1024

EPS = 1e-6
H = 8
DH = 128
DQ_C = 64
SC_W = 4
FFN_CONV_W = 3
DELTA_CHUNK = 64
REL_BUCKETS = 32
REL_MAX_DIST = 128
PAGE = 128

A_CONV = 3 * H * DH
QKV_A, Z_A, Q_B, K_B, V_B, SMALL = 0, 3072, 4096, 5120, 6144, 7168
EVEN_N = 7680
BETA_L, G_L, LOGF_L = 0, 8, 16


def _t5_bucket_starts():
    exact = REL_BUCKETS // 2
    d = np.arange(REL_MAX_DIST + 1)
    far = exact + (np.log(np.maximum(d, 1).astype(np.float32) / exact)
                   / math.log(REL_MAX_DIST / exact) * (REL_BUCKETS - exact)).astype(np.int32)
    bucket = np.where(d < exact, d, np.minimum(far, REL_BUCKETS - 1))
    return [int(np.argmax(bucket == b)) for b in range(REL_BUCKETS)]


T5_STARTS = _t5_bucket_starts()


def _cparams(n_axes, vmem_mib):
    return pltpu.CompilerParams(dimension_semantics=("arbitrary",) * n_axes,
                                vmem_limit_bytes=vmem_mib * MIB)


def _nt(a, b, precision=None):
    return lax.dot_general(a, b, (((1,), (1,)), ((), ())), precision=precision,
                           preferred_element_type=F32)


def _softplus_neg_abs(z):
    return jnp.log1p(jnp.exp(-jnp.abs(z)))


def _t5_bias(dist, rb_ref, h):
    val = jnp.full(dist.shape, rb_ref[REL_BUCKETS - 1, h], F32)
    for b in range(REL_BUCKETS - 2, -1, -1):
        val = jnp.where(dist < T5_STARTS[b + 1], rb_ref[b, h], val)
    return val


def _ada_kernel(c_ref, w_ref, b_ref, o_ref):
    c = c_ref[...]
    a = (c * jax.nn.sigmoid(c)).astype(BF16)
    o_ref[0] = jnp.dot(a, w_ref[0].astype(BF16), preferred_element_type=F32) + b_ref[0]


def ada_mod(c_all, w_ada, b_ada):
    nl, d, n = w_ada.shape
    mp = c_all.shape[0]
    tn = 1024
    return pl.pallas_call(
        _ada_kernel,
        grid=(nl, n // tn),
        in_specs=[pl.BlockSpec((mp, d), lambda l, j: (0, 0)),
                  pl.BlockSpec((1, d, tn), lambda l, j: (l, 0, j)),
                  pl.BlockSpec((1, 1, tn), lambda l, j: (l, 0, j))],
        out_specs=pl.BlockSpec((1, mp, tn), lambda l, j: (l, 0, j)),
        out_shape=jax.ShapeDtypeStruct((nl, mp, n), F32),
        compiler_params=_cparams(2, 40),
        name="ada_mod",
    )(c_all, w_ada, b_ada.reshape(nl, 1, n))


def _nmm_kernel(x_ref, g_ref, sc_ref, sh_ref, w_ref, o_ref, h_scr):
    @pl.when(pl.program_id(1) == 0)
    def _():
        x = x_ref[...]
        ms = jnp.mean(x * x, axis=-1, keepdims=True)
        y = x * lax.rsqrt(ms + EPS) * g_ref[...]
        h_scr[...] = (y * (1.0 + sc_ref[0]) + sh_ref[0]).astype(BF16)

    o_ref[...] = jnp.dot(h_scr[...], w_ref[...], preferred_element_type=F32)


def norm_mod_matmul(x, g, sc, sh, w, seq_rows, tm, tn):
    m, d = x.shape
    n = w.shape[1]
    r = sc.shape[1]
    if r == 1:
        mod_idx = lambda i, j: ((i * tm) // seq_rows, 0, 0)
    else:
        mod_idx = lambda i, j: (i, 0, 0)
    return pl.pallas_call(
        _nmm_kernel,
        grid=(m // tm, n // tn),
        in_specs=[pl.BlockSpec((tm, d), lambda i, j: (i, 0)),
                  pl.BlockSpec((1, d), lambda i, j: (0, 0)),
                  pl.BlockSpec((1, r, d), mod_idx),
                  pl.BlockSpec((1, r, d), mod_idx),
                  pl.BlockSpec((d, tn), lambda i, j: (0, j))],
        out_specs=pl.BlockSpec((tm, tn), lambda i, j: (i, j)),
        out_shape=jax.ShapeDtypeStruct((m, n), F32),
        scratch_shapes=[pltpu.VMEM((tm, d), BF16)],
        compiler_params=_cparams(2, 48),
        name="norm_mod_matmul",
    )(x, g.reshape(1, d), sc, sh, w)


def _mmres_kernel(*refs, n_a):
    a_refs = refs[:n_a]
    w_refs = refs[n_a:2 * n_a]
    xres_ref, gate_ref, o_ref = refs[2 * n_a:]
    acc = jnp.dot(a_refs[0][...].astype(BF16), w_refs[0][...], preferred_element_type=F32)
    for a_ref, w_ref in zip(a_refs[1:], w_refs[1:]):
        acc = acc + jnp.dot(a_ref[...].astype(BF16), w_ref[...], preferred_element_type=F32)
    o_ref[...] = xres_ref[...] + gate_ref[0] * acc


def matmul_gated_residual(a_list, w, xres, gate, seq_rows, tm, tn):
    m, n = xres.shape
    n_a = len(a_list)
    kp = a_list[0].shape[1]
    r = gate.shape[1]
    if r == 1:
        mod_idx = lambda i, j: ((i * tm) // seq_rows, 0, 0)
    else:
        mod_idx = lambda i, j: (i, 0, 0)
    in_specs = [pl.BlockSpec((tm, kp), lambda i, j: (i, 0)) for _ in range(n_a)]
    in_specs += [pl.BlockSpec((kp, tn), functools.partial(lambda i, j, p: (p, j), p=p)) for p in range(n_a)]
    in_specs += [pl.BlockSpec((tm, tn), lambda i, j: (i, j)),
                 pl.BlockSpec((1, r, tn), lambda i, j: mod_idx(i, j)[:2] + (j,))]
    return pl.pallas_call(
        functools.partial(_mmres_kernel, n_a=n_a),
        grid=(m // tm, n // tn),
        in_specs=in_specs,
        out_specs=pl.BlockSpec((tm, tn), lambda i, j: (i, j)),
        out_shape=jax.ShapeDtypeStruct((m, n), F32),
        compiler_params=_cparams(2, 48),
        name="matmul_gated_residual",
    )(*a_list, *([w] * n_a), xres, gate)


def _conv_head(xh, buf_ref, w, width):
    nb = buf_ref.shape[0]
    c = xh.shape[-1]
    t = lax.broadcasted_iota(jnp.int32, xh.shape, 0) % SUBLANES
    y = xh * w[width - 1:width]
    for s in range(1, width):
        sh = pltpu.roll(xh, s, 0)
        for tt in range(s):
            k = tt + width - 1 - s
            brow = buf_ref[:, k:k + 1, :]
            bsel = jnp.broadcast_to(brow, (nb, SUBLANES, c)).reshape(nb * SUBLANES, c)
            sh = jnp.where(t == tt, bsel, sh)
        y = y + sh * w[width - 1 - s:width - s]
    return y


def _conv_bulk(x2, w, width):
    y = x2 * w[width - 1:width]
    for s in range(1, width):
        y = y + pltpu.roll(x2, s, 0) * w[width - 1 - s:width - s]
    return y


def _conv_apply(x_ref, buf_ref, w_ref, t_len, width, emit):
    nb = x_ref.shape[0]
    c = x_ref.shape[2]
    w = w_ref[...]
    if t_len == SUBLANES:
        xh = x_ref[...].reshape(nb * SUBLANES, c)
        emit(None, _conv_head(xh, buf_ref, w, width))
    else:
        assert nb == 1
        emit(None, _conv_bulk(x_ref[0], w, width))
        emit(SUBLANES, _conv_head(x_ref[0, 0:SUBLANES, :], buf_ref, w, width))


def _dconv_kernel(x_ref, buf_ref, w_ref, o_ref, *, t_len):
    j = pl.program_id(1)
    nb = x_ref.shape[0]

    def emit(rows, y):
        y = y * jax.nn.sigmoid(y)
        nrm = y * lax.rsqrt(jnp.sum(y * y, axis=-1, keepdims=True) + EPS)
        y = jnp.where(j < H, nrm * (DH ** -0.5), jnp.where(j < 2 * H, nrm, y))
        if rows is None:
            o_ref[...] = y.reshape(o_ref.shape)
        else:
            o_ref[0, 0:rows, :] = y

    _conv_apply(x_ref, buf_ref, w_ref, t_len, SC_W, emit)


def delta_conv(proj3, buf, w, nb):
    b, t, _ = proj3.shape
    return pl.pallas_call(
        functools.partial(_dconv_kernel, t_len=t),
        grid=(b // nb, A_CONV // LANES),
        in_specs=[pl.BlockSpec((nb, t, LANES), lambda i, j: (i, 0, j)),
                  pl.BlockSpec((nb, SC_W - 1, LANES), lambda i, j: (i, 0, j)),
                  pl.BlockSpec((SC_W, LANES), lambda i, j: (0, j))],
        out_specs=pl.BlockSpec((nb, t, LANES), lambda i, j: (i, 0, j)),
        out_shape=jax.ShapeDtypeStruct((b, t, A_CONV), F32),
        compiler_params=_cparams(2, 32),
        name="delta_conv",
    )(proj3, buf, w)


def _ffn_act_kernel(xg_ref, xu_ref, bg_ref, bu_ref, wg_ref, wu_ref, cg_ref, cu_ref, o_ref, *, t_len):
    outs = {}

    def emit_g(rows, y):
        outs[("g", rows)] = y + cg_ref[...]

    def emit_u(rows, y):
        outs[("u", rows)] = y + cu_ref[...]

    _conv_apply(xg_ref, bg_ref, wg_ref, t_len, FFN_CONV_W, emit_g)
    _conv_apply(xu_ref, bu_ref, wu_ref, t_len, FFN_CONV_W, emit_u)
    for (kind, rows), gate in outs.items():
        if kind != "g":
            continue
        y = (gate * jax.nn.sigmoid(gate) * outs[("u", rows)]).astype(o_ref.dtype)
        if rows is None:
            o_ref[...] = y.reshape(o_ref.shape)
        else:
            o_ref[0, 0:rows, :] = y


def ffn_act(u3, buf, w, bias, nb, tc):
    b, t, f2 = u3.shape
    f = f2 // 2
    nj = f // tc
    bias2 = bias.reshape(1, f2)
    return pl.pallas_call(
        functools.partial(_ffn_act_kernel, t_len=t),
        grid=(b // nb, nj),
        in_specs=[pl.BlockSpec((nb, t, tc), lambda i, j: (i, 0, j)),
                  pl.BlockSpec((nb, t, tc), lambda i, j: (i, 0, j + nj)),
                  pl.BlockSpec((nb, FFN_CONV_W - 1, tc), lambda i, j: (i, 0, j)),
                  pl.BlockSpec((nb, FFN_CONV_W - 1, tc), lambda i, j: (i, 0, j + nj)),
                  pl.BlockSpec((FFN_CONV_W, tc), lambda i, j: (0, j)),
                  pl.BlockSpec((FFN_CONV_W, tc), lambda i, j: (0, j + nj)),
                  pl.BlockSpec((1, tc), lambda i, j: (0, j)),
                  pl.BlockSpec((1, tc), lambda i, j: (0, j + nj))],
        out_specs=pl.BlockSpec((nb, t, tc), lambda i, j: (i, 0, j)),
        out_shape=jax.ShapeDtypeStruct((b, t, f), F32),
        compiler_params=_cparams(2, 48),
        name="ffn_act",
    )(u3, u3, buf, buf, w, w, bias2, bias2)


def _hnorm_kernel(x_ref, g_ref, o_ref, *, group):
    x = x_ref[...]
    s = x * x
    if group == LANES:
        ms = jnp.mean(s, axis=-1, keepdims=True)
    else:
        lane = lax.broadcasted_iota(jnp.int32, x.shape, 1)
        lo = jnp.sum(jnp.where(lane < group, s, 0.0), axis=-1, keepdims=True)
        hi = jnp.sum(jnp.where(lane >= group, s, 0.0), axis=-1, keepdims=True)
        ms = jnp.where(lane < group, lo, hi) * (1.0 / group)
    o_ref[...] = x * lax.rsqrt(ms + EPS) * g_ref[...]


def head_norm(proj, col0, ncols, gain, tm):
    m = proj.shape[0]
    group = gain.shape[0]
    g = jnp.tile(gain, LANES // group).reshape(1, LANES)
    cb = col0 // LANES
    return pl.pallas_call(
        functools.partial(_hnorm_kernel, group=group),
        grid=(m // tm, ncols // LANES),
        in_specs=[pl.BlockSpec((tm, LANES), lambda i, j: (i, cb + j)),
                  pl.BlockSpec((1, LANES), lambda i, j: (0, 0))],
        out_specs=pl.BlockSpec((tm, LANES), lambda i, j: (i, j)),
        out_shape=jax.ShapeDtypeStruct((m, ncols), F32),
        compiler_params=_cparams(2, 32),
        name="head_norm",
    )(proj, g)


def _hsplit_kernel(x_ref, g_ref, *o_refs, group):
    x = x_ref[...]
    if group:
        s = x * x
        if group == LANES:
            ms = jnp.mean(s, axis=-1, keepdims=True)
        else:
            lane = lax.broadcasted_iota(jnp.int32, x.shape, 1)
            lo = jnp.sum(jnp.where(lane < group, s, 0.0), axis=-1, keepdims=True)
            hi = jnp.sum(jnp.where(lane >= group, s, 0.0), axis=-1, keepdims=True)
            ms = jnp.where(lane < group, lo, hi) * (1.0 / group)
        x = x * lax.rsqrt(ms + EPS) * g_ref[...]
    for o_ref in o_refs:
        o_ref[...] = x.reshape(o_ref.shape)


def head_major(proj, col0, gain, tm, b, t, token_major):
    m = proj.shape[0]
    group = 0 if gain is None else gain.shape[0]
    g = jnp.ones((1, LANES), F32) if gain is None else jnp.tile(gain, LANES // group).reshape(1, LANES)
    cb = col0 // LANES
    nt = t // tm
    out_specs = [pl.BlockSpec((1, 1, tm, DH), lambda i, j: (i // nt, j, i % nt, 0))]
    out_shape = [jax.ShapeDtypeStruct((b, H, t, DH), F32)]
    if token_major:
        out_specs.append(pl.BlockSpec((tm, LANES), lambda i, j: (i, j)))
        out_shape.append(jax.ShapeDtypeStruct((m, H * DH), F32))
    return pl.pallas_call(
        functools.partial(_hsplit_kernel, group=group),
        grid=(m // tm, H),
        in_specs=[pl.BlockSpec((tm, LANES), lambda i, j: (i, cb + j)),
                  pl.BlockSpec((1, LANES), lambda i, j: (0, 0))],
        out_specs=out_specs,
        out_shape=out_shape,
        compiler_params=_cparams(2, 32),
        name="head_major",
    )(proj, g)


def _onorm_kernel(o_ref, z_ref, g_ref, y_ref):
    o = o_ref[...]
    z = z_ref[...]
    y = o * lax.rsqrt(jnp.mean(o * o, axis=-1, keepdims=True) + EPS) * g_ref[...]
    y_ref[...] = y * (z * jax.nn.sigmoid(z))


def onorm_gate(o, proj, gain, tm):
    m = o.shape[0]
    zb = Z_A // LANES
    return pl.pallas_call(
        _onorm_kernel,
        grid=(m // tm, H),
        in_specs=[pl.BlockSpec((tm, LANES), lambda i, j: (i, j)),
                  pl.BlockSpec((tm, LANES), lambda i, j: (i, zb + j)),
                  pl.BlockSpec((1, LANES), lambda i, j: (0, 0))],
        out_specs=pl.BlockSpec((tm, LANES), lambda i, j: (i, j)),
        out_shape=jax.ShapeDtypeStruct((m, H * DH), F32),
        compiler_params=_cparams(2, 32),
        name="onorm_gate",
    )(o, proj, gain.reshape(1, LANES))


def _gates_kernel(s_ref, bias_ref, alog_ref, g_ref, c_ref, carry, *, tm):
    @pl.when(pl.program_id(1) == 0)
    def _():
        carry[...] = jnp.zeros_like(carry)

    t = s_ref[0] + bias_ref[...]
    lane = lax.broadcasted_iota(jnp.int32, t.shape, 1)
    row = lax.broadcasted_iota(jnp.int32, t.shape, 0)
    l1p = _softplus_neg_abs(t)
    sig = jax.nn.sigmoid(t)
    g = -jnp.exp(alog_ref[...]) * (jnp.maximum(t, 0.0) + l1p)
    lsig = jnp.minimum(t, 0.0) - l1p
    gt = jnp.where(lane < G_L, sig, jnp.where(lane < LOGF_L, g, lsig))
    gt = jnp.where(lane < LOGF_L + H, gt, 0.0)
    g_ref[0] = gt
    cs = gt
    k = 1
    while k < tm:
        cs = cs + jnp.where(row >= k, pltpu.roll(cs, k, 0), 0.0)
        k *= 2
    cs = cs + carry[0:1, :]
    c_ref[0] = cs
    carry[...] = jnp.broadcast_to(cs[tm - 1:tm, :], carry.shape)


def even_gates(proj3, bias_vec, alog_vec, tm):
    b, t, _ = proj3.shape
    sb = SMALL // LANES
    return pl.pallas_call(
        functools.partial(_gates_kernel, tm=tm),
        grid=(b, t // tm),
        in_specs=[pl.BlockSpec((1, tm, LANES), lambda i, j: (i, j, sb)),
                  pl.BlockSpec((1, LANES), lambda i, j: (0, 0)),
                  pl.BlockSpec((1, LANES), lambda i, j: (0, 0))],
        out_specs=[pl.BlockSpec((1, tm, LANES), lambda i, j: (i, j, 0)),
                   pl.BlockSpec((1, tm, LANES), lambda i, j: (i, j, 0))],
        out_shape=[jax.ShapeDtypeStruct((b, t, LANES), F32),
                   jax.ShapeDtypeStruct((b, t, LANES), F32)],
        scratch_shapes=[pltpu.VMEM((SUBLANES, LANES), F32)],
        compiler_params=_cparams(2, 32),
        name="even_gates",
    )(proj3, bias_vec, alog_vec)


def _bmm(a, b, spec):
    return jnp.einsum(spec, a, b, precision=HI, preferred_element_type=F32)


def _delta_wy(q3, k3, v3, beta, g3):
    n, L, _ = q3.shape
    ii = lax.broadcasted_iota(jnp.int32, (n, L, L), 1)
    jj = lax.broadcasted_iota(jnp.int32, (n, L, L), 2)
    g_row = jnp.sum(jnp.where(ii == jj, jnp.broadcast_to(g3, (n, L, L)), 0.0), axis=1, keepdims=True)
    decay = jnp.where(ii >= jj, jnp.exp(jnp.where(ii >= jj, g3 - g_row, 0.0)), 0.0)
    kb3 = k3 * beta
    a = jnp.where(ii > jj, _bmm(kb3, k3, 'cid,cjd->cij') * decay, 0.0)
    pk = -a
    tinv = (ii == jj).astype(F32) + pk
    span = 2
    while span < L:
        pk = _bmm(pk, pk, 'cij,cjk->cik')
        tinv = tinv + _bmm(tinv, pk, 'cij,cjk->cik')
        span *= 2
    eg = jnp.exp(g3)
    u0 = _bmm(tinv, v3 * beta, 'cij,cjd->cid')
    wk = _bmm(tinv, kb3 * eg, 'cij,cjd->cid')
    qk = _bmm(q3, k3, 'cid,cjd->cij') * decay
    glast = g3[:, L - 1:L, :]
    return u0, wk, qk, k3 * jnp.exp(glast - g3), q3 * eg, jnp.exp(glast)


def _delta_heads_kernel(qkv_ref, gt_ref, s0_ref, o_ref, sn_ref):
    L = DELTA_CHUNK
    gt = gt_ref[0]
    row = lax.broadcasted_iota(jnp.int32, gt.shape, 0)
    cs = gt
    step = 1
    while step < L:
        cs = cs + jnp.where(row >= step, pltpu.roll(cs, step, 0), 0.0)
        step *= 2
    heads = lambda part: jnp.stack([qkv_ref[0, :, (part * H + h) * DH:(part * H + h + 1) * DH] for h in range(H)])
    beta = jnp.stack([gt[:, BETA_L + h:BETA_L + h + 1] for h in range(H)])
    g3 = jnp.stack([cs[:, G_L + h:G_L + h + 1] for h in range(H)])
    u0, wk, qk, kd, qg, egl = _delta_wy(heads(0), heads(1), heads(2), beta, g3)
    s0 = s0_ref[0]
    v_new = u0 - _bmm(wk, s0, 'hld,hdv->hlv')
    o = _bmm(qg, s0, 'hld,hdv->hlv') + _bmm(qk, v_new, 'hij,hjv->hiv')
    for h in range(H):
        o_ref[0, :, h * DH:(h + 1) * DH] = o[h]
        sn_ref[0, h] = s0[h] * egl[h] + lax.dot_general(kd[h], v_new[h], (((0,), (0,)), ((), ())),
                                                        precision=HI, preferred_element_type=F32)


def delta_rule_one_chunk(qkv3, gates, s0):
    b, t, _ = qkv3.shape
    assert t == DELTA_CHUNK
    return pl.pallas_call(
        _delta_heads_kernel,
        grid=(b,),
        in_specs=[pl.BlockSpec((1, t, A_CONV), lambda i: (i, 0, 0)),
                  pl.BlockSpec((1, t, LANES), lambda i: (i, 0, 0)),
                  pl.BlockSpec((1, H, DH, DH), lambda i: (i, 0, 0, 0))],
        out_specs=[pl.BlockSpec((1, t, H * DH), lambda i: (i, 0, 0)),
                   pl.BlockSpec((1, H, DH, DH), lambda i: (i, 0, 0, 0))],
        out_shape=[jax.ShapeDtypeStruct((b, t, H * DH), F32),
                   jax.ShapeDtypeStruct((b, H, DH, DH), F32)],
        compiler_params=_cparams(1, 32),
        name="delta_rule_one_chunk",
    )(qkv3, gates, s0)


def _delta_kernel(q_ref, k_ref, v_ref, gt_ref, s0_ref, o_ref, sn_ref,
                  wk_s, u0_s, qg_s, kd_s, qk_s, eg_s, *, t_len):
    L = DELTA_CHUNK
    nc = t_len // L
    h = pl.program_id(1)
    gt = gt_ref[0]
    lane = lax.broadcasted_iota(jnp.int32, gt.shape, 1)
    pos = lax.broadcasted_iota(jnp.int32, gt.shape, 0) % L
    onehot = (lane == G_L + h).astype(F32)
    cs = gt * onehot
    step = 1
    while step < L:
        cs = cs + jnp.where(pos >= step, pltpu.roll(cs, step, 0), 0.0)
        step *= 2
    gt3 = gt.reshape(nc, L, LANES)
    lane3 = lax.broadcasted_iota(jnp.int32, gt3.shape, 2)
    beta = jnp.sum(jnp.where(lane3 == BETA_L + h, gt3, 0.0), axis=-1, keepdims=True)
    g3 = jnp.sum(cs.reshape(nc, L, LANES), axis=-1, keepdims=True)
    u0, wk, qk, kd, qg, egl = _delta_wy(q_ref[0].reshape(nc, L, DH), k_ref[0].reshape(nc, L, DH),
                                        v_ref[0].reshape(nc, L, DH), beta, g3)
    u0_s[...] = u0
    wk_s[...] = wk
    qk_s[...] = qk
    kd_s[...] = kd
    qg_s[...] = qg
    eg_s[...] = jnp.broadcast_to(egl, (nc, 1, LANES))

    def body(c, s):
        v_new = u0_s[c] - jnp.dot(wk_s[c], s, precision=HI, preferred_element_type=F32)
        o = (jnp.dot(qg_s[c], s, precision=HI, preferred_element_type=F32)
             + jnp.dot(qk_s[c], v_new, precision=HI, preferred_element_type=F32))
        o_ref[0, pl.ds(pl.multiple_of(c * L, L), L), :] = o
        return s * eg_s[c] + lax.dot_general(kd_s[c], v_new, (((0,), (0,)), ((), ())),
                                             precision=HI, preferred_element_type=F32)

    sn_ref[0, 0] = lax.fori_loop(0, nc, body, s0_ref[0, 0])


def delta_rule(qkv3, gates, s0):
    b, t, _ = qkv3.shape
    nc = t // DELTA_CHUNK
    L = DELTA_CHUNK
    return pl.pallas_call(
        functools.partial(_delta_kernel, t_len=t),
        grid=(b, H),
        in_specs=[pl.BlockSpec((1, t, DH), lambda i, j: (i, 0, j)),
                  pl.BlockSpec((1, t, DH), lambda i, j: (i, 0, H + j)),
                  pl.BlockSpec((1, t, DH), lambda i, j: (i, 0, 2 * H + j)),
                  pl.BlockSpec((1, t, LANES), lambda i, j: (i, 0, 0)),
                  pl.BlockSpec((1, 1, DH, DH), lambda i, j: (i, j, 0, 0))],
        out_specs=[pl.BlockSpec((1, t, DH), lambda i, j: (i, 0, j)),
                   pl.BlockSpec((1, 1, DH, DH), lambda i, j: (i, j, 0, 0))],
        out_shape=[jax.ShapeDtypeStruct((b, t, H * DH), F32),
                   jax.ShapeDtypeStruct((b, H, DH, DH), F32)],
        scratch_shapes=[pltpu.VMEM((nc, L, DH), F32), pltpu.VMEM((nc, L, DH), F32),
                        pltpu.VMEM((nc, L, DH), F32), pltpu.VMEM((nc, L, DH), F32),
                        pltpu.VMEM((nc, L, L), F32), pltpu.VMEM((nc, 1, LANES), F32)],
        compiler_params=_cparams(2, 48),
        name="delta_rule",
    )(qkv3, qkv3, qkv3, gates, s0)


def _last_kblock(qi, tq, tk):
    return ((qi + 1) * tq - 1) // tk


ROW_SUBTILES = 2


def _row_subtiles(rows):
    n = ROW_SUBTILES if rows % (ROW_SUBTILES * SUBLANES) == 0 else 1
    return [(i * rows // n, (i + 1) * rows // n) for i in range(n)]


def _online_softmax(s, v_bf, m_ref, l_ref, acc_ref):
    m_prev = m_ref[...]
    m_new = jnp.maximum(m_prev, jnp.max(s, axis=-1, keepdims=True))
    alpha = jnp.exp(m_prev - m_new)
    p = jnp.exp(s - m_new)
    l_ref[...] = alpha * l_ref[...] + jnp.sum(p, axis=-1, keepdims=True)
    acc_ref[...] = alpha * acc_ref[...] + jnp.dot(p.astype(BF16), v_bf, preferred_element_type=F32)
    m_ref[...] = m_new


def _fox_p_kernel(q_ref, k_ref, v_ref, cq_ref, ck_ref, o_ref, m_s, l_s, acc_s, *, tq, tk):
    h = pl.program_id(1)
    qi = pl.program_id(2)
    ki = pl.program_id(3)
    last = _last_kblock(qi, tq, tk)

    @pl.when(ki == 0)
    def _():
        m_s[...] = jnp.full(m_s.shape, -jnp.inf, F32)
        l_s[...] = jnp.zeros_like(l_s)
        acc_s[...] = jnp.zeros_like(acc_s)

    def step(masked):
        k = k_ref[0, 0].astype(BF16)
        v = v_ref[0, 0].astype(BF16)
        ck = ck_ref[0, 0]
        for r0, r1 in _row_subtiles(tq):
            s = _nt(q_ref[0, 0, r0:r1].astype(BF16), k) * (DH ** -0.5)
            cqb = cq_ref[0, r0:r1]
            lane = lax.broadcasted_iota(jnp.int32, cqb.shape, 1)
            cq = jnp.sum(jnp.where(lane == LOGF_L + h, cqb, 0.0), axis=-1, keepdims=True)
            s = s + cq - ck
            if masked:
                qpos = qi * tq + r0 + lax.broadcasted_iota(jnp.int32, s.shape, 0)
                kpos = ki * tk + lax.broadcasted_iota(jnp.int32, s.shape, 1)
                s = jnp.where(kpos <= qpos, s, -jnp.inf)
            _online_softmax(s, v, m_s.at[r0:r1], l_s.at[r0:r1], acc_s.at[r0:r1])

    straddles = (ki + 1) * tk - 1 > qi * tq

    @pl.when(jnp.logical_and(ki <= last, straddles))
    def _():
        step(True)

    @pl.when(jnp.logical_and(ki <= last, jnp.logical_not(straddles)))
    def _():
        step(False)

    @pl.when(ki == last)
    def _():
        o_ref[0] = acc_s[...] / l_s[...]


def fox_prompt(q4, k4, v4, cum, cum_t, tq, tk):
    b, _, t, _ = q4.shape
    kidx = lambda bi, h, qi, ki: jnp.minimum(ki, _last_kblock(qi, tq, tk))
    return pl.pallas_call(
        functools.partial(_fox_p_kernel, tq=tq, tk=tk),
        grid=(b, H, t // tq, t // tk),
        in_specs=[pl.BlockSpec((1, 1, tq, DH), lambda bi, h, qi, ki: (bi, h, qi, 0)),
                  pl.BlockSpec((1, 1, tk, DH), lambda bi, h, qi, ki: (bi, h, kidx(bi, h, qi, ki), 0)),
                  pl.BlockSpec((1, 1, tk, DH), lambda bi, h, qi, ki: (bi, h, kidx(bi, h, qi, ki), 0)),
                  pl.BlockSpec((1, tq, LANES), lambda bi, h, qi, ki: (bi, qi, 0)),
                  pl.BlockSpec((1, 1, 1, tk), lambda bi, h, qi, ki: (bi, h, 0, kidx(bi, h, qi, ki)))],
        out_specs=pl.BlockSpec((1, tq, DH), lambda bi, h, qi, ki: (bi, qi, h)),
        out_shape=jax.ShapeDtypeStruct((b, t, H * DH), F32),
        scratch_shapes=[pltpu.VMEM((tq, 1), F32), pltpu.VMEM((tq, 1), F32), pltpu.VMEM((tq, DH), F32)],
        compiler_params=_cparams(4, 32),
        name="fox_prompt",
    )(q4, k4, v4, cum, cum_t)


def _t5_tile_kernel(rb_ref, o_ref, *, tb):
    h = pl.program_id(0)
    off = pl.program_id(1)
    dist = (off * tb + lax.broadcasted_iota(jnp.int32, (tb, tb), 0)
            - lax.broadcasted_iota(jnp.int32, (tb, tb), 1))
    o_ref[0, 0] = _t5_bias(dist, rb_ref, h)


def t5_tiles(rel_bias, tb):
    return pl.pallas_call(
        functools.partial(_t5_tile_kernel, tb=tb),
        grid=(H, 2),
        in_specs=[pl.BlockSpec(memory_space=pltpu.SMEM)],
        out_specs=pl.BlockSpec((1, 1, tb, tb), lambda h, o: (h, o, 0, 0)),
        out_shape=jax.ShapeDtypeStruct((H, 2, tb, tb), F32),
        compiler_params=_cparams(2, 32),
        name="t5_tiles",
    )(rel_bias)


def _diff_lambda(lam_ref, lam_init):
    lp = lam_ref[...]
    s01 = jnp.sum(lp[0:1] * lp[1:2], axis=-1, keepdims=True)
    s23 = jnp.sum(lp[2:3] * lp[3:4], axis=-1, keepdims=True)
    return jnp.exp(s01) - jnp.exp(s23) + lam_init


def _diff_finish(acc0, l0, acc1, l1, lam, g, lam_init):
    o = acc0 / l0 - lam * (acc1 / l1)
    return o * lax.rsqrt(jnp.mean(o * o, axis=-1, keepdims=True) + EPS) * g * (1.0 - lam_init)


def _split_halves(qf):
    lane = lax.broadcasted_iota(jnp.int32, qf.shape, 1)
    return jnp.concatenate([jnp.where(lane < DQ_C, qf, 0.0), jnp.where(lane >= DQ_C, qf, 0.0)], axis=0)


def _diff_p_kernel(rb_ref, q_ref, k_ref, v_ref, bias_ref, lam_ref, g_ref, o_ref, m_s, l_s, acc_s,
                   *, tb, lam_init):
    h = pl.program_id(1)
    qi = pl.program_id(2)
    ki = pl.program_id(3)

    @pl.when(ki == 0)
    def _():
        m_s[...] = jnp.full(m_s.shape, -jnp.inf, F32)
        l_s[...] = jnp.zeros_like(l_s)
        acc_s[...] = jnp.zeros_like(acc_s)

    def step(on_diagonal):
        q2 = _split_halves(q_ref[0, 0]).astype(BF16)
        near = (qi - ki) <= 1
        bias = jnp.where(near, bias_ref[0, 0], rb_ref[REL_BUCKETS - 1, h])
        s = _nt(q2, k_ref[0, 0].astype(BF16)) * (DQ_C ** -0.5) + jnp.concatenate([bias, bias], axis=0)
        if on_diagonal:
            row = lax.broadcasted_iota(jnp.int32, s.shape, 0)
            qpos = jnp.where(row >= tb, row - tb, row)
            s = jnp.where(lax.broadcasted_iota(jnp.int32, s.shape, 1) <= qpos, s, -jnp.inf)
        _online_softmax(s, v_ref[0, 0].astype(BF16), m_s, l_s, acc_s)

    @pl.when(ki < qi)
    def _():
        step(False)

    @pl.when(ki == qi)
    def _():
        step(True)
        lam = _diff_lambda(lam_ref, lam_init)
        o_ref[0] = _diff_finish(acc_s[0:tb], l_s[0:tb], acc_s[tb:2 * tb], l_s[tb:2 * tb], lam, g_ref[...], lam_init)


def diff_prompt(q4, k4, v4, tiles, rel_bias, lam_p, subln_g, lam_init, tb):
    b, _, t, _ = q4.shape
    kidx = lambda qi, ki: jnp.minimum(ki, qi)
    return pl.pallas_call(
        functools.partial(_diff_p_kernel, tb=tb, lam_init=lam_init),
        grid=(b, H, t // tb, t // tb),
        in_specs=[pl.BlockSpec(memory_space=pltpu.SMEM),
                  pl.BlockSpec((1, 1, tb, DH), lambda bi, h, qi, ki: (bi, h, qi, 0)),
                  pl.BlockSpec((1, 1, tb, DH), lambda bi, h, qi, ki: (bi, h, kidx(qi, ki), 0)),
                  pl.BlockSpec((1, 1, tb, DH), lambda bi, h, qi, ki: (bi, h, kidx(qi, ki), 0)),
                  pl.BlockSpec((1, 1, tb, tb), lambda bi, h, qi, ki: (h, jnp.clip(qi - ki, 0, 1), 0, 0)),
                  pl.BlockSpec((4, DQ_C), lambda bi, h, qi, ki: (0, 0)),
                  pl.BlockSpec((1, DH), lambda bi, h, qi, ki: (0, 0))],
        out_specs=pl.BlockSpec((1, tb, DH), lambda bi, h, qi, ki: (bi, qi, h)),
        out_shape=jax.ShapeDtypeStruct((b, t, H * DH), F32),
        scratch_shapes=[pltpu.VMEM((2 * tb, 1), F32), pltpu.VMEM((2 * tb, 1), F32), pltpu.VMEM((2 * tb, DH), F32)],
        compiler_params=_cparams(4, 40),
        name="diff_prompt",
    )(rel_bias, q4, k4, v4, tiles, lam_p, subln_g.reshape(1, DH))


def _suffix_matrix(n):
    return (lax.broadcasted_iota(jnp.int32, (n, n), 0) > lax.broadcasted_iota(jnp.int32, (n, n), 1)).astype(BF16)


def _rows_times_01(x, u, parts):
    m = x.shape[0]
    terms = []
    rest = x
    for _ in range(parts):
        term = rest.astype(BF16).astype(F32)
        terms.append(term)
        rest = rest - term
    y = jnp.dot(jnp.concatenate(terms, axis=0).astype(BF16), u, preferred_element_type=F32)
    out = y[0:m]
    for i in range(1, parts):
        out = out + y[i * m:(i + 1) * m]
    return out


SB_UNDERFLOW = -104.0


def _sb_logs(z):
    l1p = _softplus_neg_abs(z)
    return jnp.minimum(z, 0.0) - l1p, jnp.minimum(-z, 0.0) - l1p


def _sb_p_kernel(q_ref, k_ref, v_ref, o_ref, carry_s, acc_s, *, tq, tk):
    qi = pl.program_id(2)
    ki = pl.program_id(3)
    last = _last_kblock(qi, tq, tk)
    kb = last - ki

    @pl.when(ki == 0)
    def _():
        carry_s[...] = jnp.zeros_like(carry_s)
        acc_s[...] = jnp.zeros_like(acc_s)

    def step(masked):
        z = _nt(q_ref[0, 0].astype(BF16), k_ref[0, 0].astype(BF16)) * (DH ** -0.5)
        log_sig, log_keep = _sb_logs(z)
        if masked:
            qpos = qi * tq + lax.broadcasted_iota(jnp.int32, z.shape, 0)
            kpos = kb * tk + lax.broadcasted_iota(jnp.int32, z.shape, 1)
            mask = kpos < qpos
            log_keep = jnp.where(mask, log_keep, 0.0)
        later = carry_s[...] + _rows_times_01(log_keep, _suffix_matrix(tk), 2)
        a = jnp.exp(log_sig + later)
        if masked:
            a = jnp.where(mask, a, 0.0)
        acc_s[...] += jnp.dot(a.astype(BF16), v_ref[0, 0].astype(BF16), preferred_element_type=F32)
        carry_s[...] += jnp.sum(log_keep, axis=-1, keepdims=True)

    live = jnp.logical_and(ki <= last, jnp.max(carry_s[...]) > SB_UNDERFLOW)
    straddles = (kb + 1) * tk > qi * tq

    @pl.when(jnp.logical_and(live, straddles))
    def _():
        step(True)

    @pl.when(jnp.logical_and(live, jnp.logical_not(straddles)))
    def _():
        step(False)

    @pl.when(ki == last)
    def _():
        o_ref[0] = acc_s[...]


def sb_prompt(q4, k4, v4, tq, tk):
    b, _, t, _ = q4.shape
    kidx = lambda qi, ki: jnp.maximum(_last_kblock(qi, tq, tk) - ki, 0)
    return pl.pallas_call(
        functools.partial(_sb_p_kernel, tq=tq, tk=tk),
        grid=(b, H, t // tq, t // tk),
        in_specs=[pl.BlockSpec((1, 1, tq, DH), lambda bi, h, qi, ki: (bi, h, qi, 0)),
                  pl.BlockSpec((1, 1, tk, DH), lambda bi, h, qi, ki: (bi, h, kidx(qi, ki), 0)),
                  pl.BlockSpec((1, 1, tk, DH), lambda bi, h, qi, ki: (bi, h, kidx(qi, ki), 0))],
        out_specs=pl.BlockSpec((1, tq, DH), lambda bi, h, qi, ki: (bi, qi, h)),
        out_shape=jax.ShapeDtypeStruct((b, t, H * DH), F32),
        scratch_shapes=[pltpu.VMEM((tq, 1), F32), pltpu.VMEM((tq, DH), F32)],
        compiler_params=_cparams(4, 32),
        name="sb_prompt",
    )(q4, k4, v4)


PAGES_PER_STEP = 8


def _pages_per_step(n_pages):
    return max(d for d in range(1, PAGES_PER_STEP + 1) if n_pages % d == 0)


def _slot_page(pt_ref, b, p, slot, n_pages, pps, layer_off):
    return layer_off + pt_ref[b, n_pages - 1 - (jnp.maximum(p - 1, 0) * pps + slot)]


def _head_rows_bf16(refs, h):
    parts = [r[0, :, h, :].astype(BF16) for r in refs]
    return parts[0] if len(parts) == 1 else jnp.concatenate(parts, axis=0)


def _qk_heads(q_list, k_refs):
    return jnp.concatenate([_nt(q_list[h], _head_rows_bf16(k_refs, h)) for h in range(H)], axis=0)


def _pv_heads(p, v_refs, rows):
    return jnp.concatenate(
        [jnp.dot(p[h * rows:(h + 1) * rows].astype(BF16), _head_rows_bf16(v_refs, h), preferred_element_type=F32)
         for h in range(H)], axis=0)


def _softmax_step(s, v_refs, rows, m_s, l_s, acc_s):
    m_prev = m_s[...]
    m_new = jnp.maximum(m_prev, jnp.max(s, axis=-1, keepdims=True))
    alpha = jnp.exp(m_prev - m_new)
    p = jnp.exp(s - m_new)
    l_s[...] = alpha * l_s[...] + jnp.sum(p, axis=-1, keepdims=True)
    acc_s[...] = alpha * acc_s[...] + _pv_heads(p, v_refs, rows)
    m_s[...] = m_new


def _row_in_group(shape, group):
    return lax.broadcasted_iota(jnp.int32, shape, 0) % group


def _flat_pages_bf16(refs):
    parts = [r[0].reshape(PAGE * H, DH).astype(BF16) for r in refs]
    return parts[0] if len(parts) == 1 else jnp.concatenate(parts, axis=0)


def _flat_softmax_step(s, v_refs, m_s, l_s, acc_s):
    m_prev = m_s[...]
    m_new = jnp.maximum(m_prev, jnp.max(s, axis=-1, keepdims=True))
    alpha = jnp.exp(m_prev - m_new)
    p = jnp.exp(s - m_new)
    l_s[...] = alpha * l_s[...] + jnp.sum(p, axis=-1, keepdims=True)
    acc_s[...] = alpha * acc_s[...] + jnp.dot(p.astype(BF16), _flat_pages_bf16(v_refs), preferred_element_type=F32)
    m_s[...] = m_new


def _head_match(shape, rows_per_head):
    col_head = lax.broadcasted_iota(jnp.int32, shape, 1) % H
    return col_head == lax.broadcasted_iota(jnp.int32, shape, 0) // rows_per_head


def _causal_new(shape, t_len):
    return lax.broadcasted_iota(jnp.int32, shape, 1) // H <= _row_in_group(shape, t_len)


def _fox_s_kernel(pt_ref, q_ref, kn_ref, vn_ref, cumflat_ref, cumn_ref, *rest, n_pages, pps, t_len):
    kp_refs, vp_refs = rest[:pps], rest[pps:2 * pps]
    suf_refs, tot_refs = rest[2 * pps:3 * pps], rest[3 * pps:4 * pps]
    o_ref, m_s, l_s, acc_s, carry_s = rest[4 * pps:]
    p = pl.program_id(1)
    q_all = jnp.concatenate([q_ref[0, :, h * DH:(h + 1) * DH] for h in range(H)], axis=0).astype(BF16)

    def logits(k_refs, key_bias):
        s = _nt(q_all, _flat_pages_bf16(k_refs)) * (DH ** -0.5) + key_bias + cumn_ref[0]
        return s, _head_match(s.shape, t_len)

    @pl.when(p == 0)
    def _():
        m_s[...] = jnp.full(m_s.shape, -jnp.inf, F32)
        l_s[...] = jnp.zeros_like(l_s)
        acc_s[...] = jnp.zeros_like(acc_s)
        carry_s[...] = jnp.zeros_like(carry_s)
        s, ok = logits([kn_ref], -cumflat_ref[0])
        ok = jnp.logical_and(ok, _causal_new(s.shape, t_len))
        _flat_softmax_step(jnp.where(ok, s, -jnp.inf), [vn_ref], m_s, l_s, acc_s)

    @pl.when(p > 0)
    def _():
        off = carry_s[...]
        cols = []
        for i in range(pps):
            cols.append(suf_refs[i][0] + off)
            off = off + tot_refs[i][0]
        s, ok = logits(kp_refs, cols[0] if pps == 1 else jnp.concatenate(cols, axis=1))
        _flat_softmax_step(jnp.where(ok, s, -jnp.inf), vp_refs, m_s, l_s, acc_s)
        carry_s[...] = off

    @pl.when(p == n_pages // pps)
    def _():
        o = acc_s[...] / l_s[...]
        for h in range(H):
            o_ref[0, :, h * DH:(h + 1) * DH] = o[h * t_len:(h + 1) * t_len]


def _page_suffix_kernel(x_ref, suf_ref, tot_ref):
    x = x_ref[...]
    suf_ref[...] = _rows_times_01(x, _suffix_matrix(x.shape[1]), 3)
    tot_ref[...] = jnp.broadcast_to(jnp.sum(x, axis=-1, keepdims=True), x.shape)


def page_suffix(x, tm):
    r, n = x.shape
    return pl.pallas_call(
        _page_suffix_kernel,
        grid=(r // tm,),
        in_specs=[pl.BlockSpec((tm, n), lambda i: (i, 0))],
        out_specs=[pl.BlockSpec((tm, n), lambda i: (i, 0)), pl.BlockSpec((tm, n), lambda i: (i, 0))],
        out_shape=[jax.ShapeDtypeStruct((r, n), F32), jax.ShapeDtypeStruct((r, n), F32)],
        compiler_params=_cparams(1, 32),
        name="page_suffix",
    )(x)


def _pool_specs(block, n_pages, pps, layer_off):
    def idx(slot):
        return lambda b, p, pt: (_slot_page(pt, b, p, slot, n_pages, pps, layer_off),) + (0,) * (len(block) - 1)
    return [pl.BlockSpec(block, idx(slot)) for slot in range(pps)]


def _qkv_specs(t_len, n_pages, pps, layer_off):
    page_block = (1, PAGE, H, DH)
    specs = [pl.BlockSpec((1, t_len, H * DH), lambda b, p, pt: (b, 0, 0)),
             pl.BlockSpec(page_block, lambda b, p, pt: (b, 0, 0, 0)),
             pl.BlockSpec(page_block, lambda b, p, pt: (b, 0, 0, 0))]
    return (specs + _pool_specs(page_block, n_pages, pps, layer_off)
            + _pool_specs(page_block, n_pages, pps, layer_off))


def fox_sample(pt, qn3, kn4, vn4, cumn_flat, cumn_col, kpool, vpool, suf_flat, tot_flat, layer_off):
    b, t, row = qn3.shape
    n_pages = pt.shape[1]
    pps = _pages_per_step(n_pages)
    flat = PAGE * H
    specs = _qkv_specs(t, n_pages, pps, layer_off)
    in_specs = specs[:3] + [pl.BlockSpec((1, 1, flat), lambda bi, p, ptr: (bi, 0, 0)),
                            pl.BlockSpec((1, H * t, 1), lambda bi, p, ptr: (bi, 0, 0))]
    in_specs += specs[3:] + _pool_specs((1, 1, flat), n_pages, pps, layer_off)
    in_specs += _pool_specs((1, 1, flat), n_pages, pps, layer_off)
    return pl.pallas_call(
        functools.partial(_fox_s_kernel, n_pages=n_pages, pps=pps, t_len=t),
        grid_spec=pltpu.PrefetchScalarGridSpec(
            num_scalar_prefetch=1,
            grid=(b, n_pages // pps + 1),
            in_specs=in_specs,
            out_specs=pl.BlockSpec((1, t, row), lambda bi, p, ptr: (bi, 0, 0)),
            scratch_shapes=[pltpu.VMEM((H * t, 1), F32), pltpu.VMEM((H * t, 1), F32),
                            pltpu.VMEM((H * t, DH), F32), pltpu.VMEM((1, flat), F32)]),
        out_shape=jax.ShapeDtypeStruct((b, t, row), F32),
        compiler_params=_cparams(2, 40),
        name="fox_sample",
    )(pt, qn3, kn4, vn4, cumn_flat, cumn_col, *([kpool] * pps), *([vpool] * pps),
      *([suf_flat] * pps), *([tot_flat] * pps))


def _diff_s_kernel(pt_ref, rbrows_ref, q_ref, kn_ref, vn_ref, *rest, n_pages, pps, t_len, lam_init):
    kp_refs, vp_refs = rest[:pps], rest[pps:2 * pps]
    lam_ref, g_ref, o_ref, m_s, l_s, acc_s = rest[2 * pps:]
    p = pl.program_id(1)
    rows = 2 * t_len
    n_rows = H * rows
    flat = PAGE * H
    q_all = jnp.concatenate([_split_halves(q_ref[0, :, h * DH:(h + 1) * DH]) for h in range(H)],
                            axis=0).astype(BF16)
    rb = rbrows_ref[...]
    far = rb[:, REL_BUCKETS - 1:REL_BUCKETS]

    def near_bias(offset):
        shape = (n_rows, flat)
        dist = offset + _row_in_group(shape, t_len) - lax.broadcasted_iota(jnp.int32, shape, 1) // H
        val = jnp.broadcast_to(far, shape)
        for b in range(REL_BUCKETS - 2, -1, -1):
            val = jnp.where(dist < T5_STARTS[b + 1], rb[:, b:b + 1], val)
        return val

    def process(k_refs, v_refs, mode):
        n = len(k_refs)
        s = _nt(q_all, _flat_pages_bf16(k_refs)) * (DQ_C ** -0.5)
        ok = _head_match(s.shape, rows)
        if mode == "far":
            s = s + far
        else:
            bias = near_bias(0 if mode == "new" else PAGE)
            if n > 1:
                bias = jnp.concatenate([bias, jnp.broadcast_to(far, (n_rows, (n - 1) * flat))], axis=1)
            s = s + bias
        if mode == "new":
            ok = jnp.logical_and(ok, _causal_new(s.shape, t_len))
        _flat_softmax_step(jnp.where(ok, s, -jnp.inf), v_refs, m_s, l_s, acc_s)

    @pl.when(p == 0)
    def _():
        m_s[...] = jnp.full(m_s.shape, -jnp.inf, F32)
        l_s[...] = jnp.zeros_like(l_s)
        acc_s[...] = jnp.zeros_like(acc_s)
        process([kn_ref], [vn_ref], "new")

    @pl.when(p == 1)
    def _():
        process(kp_refs, vp_refs, "last")

    @pl.when(p > 1)
    def _():
        process(kp_refs, vp_refs, "far")

    @pl.when(p == n_pages // pps)
    def _():
        lam = _diff_lambda(lam_ref, lam_init)
        acc = acc_s[...]
        l = l_s[...]
        for h in range(H):
            r0, r1, r2 = h * rows, h * rows + t_len, (h + 1) * rows
            o_ref[0, :, h * DH:(h + 1) * DH] = _diff_finish(acc[r0:r1], l[r0:r1], acc[r1:r2], l[r1:r2],
                                                            lam, g_ref[...], lam_init)


def diff_sample(pt, rel_bias, qn3, kn4, vn4, kpool, vpool, lam_p, subln_g, lam_init, layer_off):
    b, t, row = qn3.shape
    n_pages = pt.shape[1]
    pps = _pages_per_step(n_pages)
    rb_rows = jnp.repeat(rel_bias.T, 2 * t, axis=0)
    in_specs = [pl.BlockSpec((2 * H * t, REL_BUCKETS), lambda bi, p, ptr: (0, 0))]
    in_specs += _qkv_specs(t, n_pages, pps, layer_off)
    in_specs += [pl.BlockSpec((4, DQ_C), lambda bi, p, ptr: (0, 0)),
                 pl.BlockSpec((1, DH), lambda bi, p, ptr: (0, 0))]
    return pl.pallas_call(
        functools.partial(_diff_s_kernel, n_pages=n_pages, pps=pps, t_len=t, lam_init=lam_init),
        grid_spec=pltpu.PrefetchScalarGridSpec(
            num_scalar_prefetch=1,
            grid=(b, n_pages // pps + 1),
            in_specs=in_specs,
            out_specs=pl.BlockSpec((1, t, row), lambda bi, p, ptr: (bi, 0, 0)),
            scratch_shapes=[pltpu.VMEM((2 * H * t, 1), F32), pltpu.VMEM((2 * H * t, 1), F32),
                            pltpu.VMEM((2 * H * t, DH), F32)]),
        out_shape=jax.ShapeDtypeStruct((b, t, row), F32),
        compiler_params=_cparams(2, 40),
        name="diff_sample",
    )(pt, rb_rows, qn3, kn4, vn4, *([kpool] * pps), *([vpool] * pps), lam_p, subln_g.reshape(1, DH))


def _sb_s_kernel(pt_ref, q_ref, kn_ref, vn_ref, *rest, n_pages, pps, t_len):
    kp_refs, vp_refs = rest[:pps], rest[pps:2 * pps]
    o_ref, carry_s, acc_s = rest[2 * pps:]
    p = pl.program_id(1)
    u = _suffix_matrix(PAGE)
    q_list = [q_ref[0, :, h * DH:(h + 1) * DH].astype(BF16) for h in range(H)]
    rows = H * t_len

    def process(k_refs, v_refs, is_new):
        n = len(k_refs)
        z = _qk_heads(q_list, k_refs) * (DH ** -0.5)
        log_sig, log_keep = _sb_logs(z)
        if is_new:
            mask = lax.broadcasted_iota(jnp.int32, z.shape, 1) < _row_in_group(z.shape, t_len)
            log_keep = jnp.where(mask, log_keep, 0.0)
        pages = [log_keep[:, i * PAGE:(i + 1) * PAGE] for i in range(n)]
        suf = _rows_times_01(pages[0] if n == 1 else jnp.concatenate(pages, axis=0), u, 2)
        off = carry_s[...]
        cols = []
        for i in range(n):
            cols.append(suf[i * rows:(i + 1) * rows] + off)
            off = off + jnp.sum(pages[i], axis=-1, keepdims=True)
        a = jnp.exp(log_sig + (cols[0] if n == 1 else jnp.concatenate(cols, axis=1)))
        if is_new:
            a = jnp.where(mask, a, 0.0)
        acc_s[...] += _pv_heads(a, v_refs, t_len)
        carry_s[...] = off

    @pl.when(p == 0)
    def _():
        carry_s[...] = jnp.zeros_like(carry_s)
        acc_s[...] = jnp.zeros_like(acc_s)
        process([kn_ref], [vn_ref], True)

    @pl.when(jnp.logical_and(p > 0, jnp.max(carry_s[...]) > SB_UNDERFLOW))
    def _():
        process(kp_refs, vp_refs, False)

    @pl.when(p == n_pages // pps)
    def _():
        acc = acc_s[...]
        for h in range(H):
            o_ref[0, :, h * DH:(h + 1) * DH] = acc[h * t_len:(h + 1) * t_len]


def sb_sample(pt, q3, kn4, vn4, kpool, vpool, layer_off):
    b, t, row = q3.shape
    n_pages = pt.shape[1]
    pps = _pages_per_step(n_pages)
    return pl.pallas_call(
        functools.partial(_sb_s_kernel, n_pages=n_pages, pps=pps, t_len=t),
        grid_spec=pltpu.PrefetchScalarGridSpec(
            num_scalar_prefetch=1,
            grid=(b, n_pages // pps + 1),
            in_specs=_qkv_specs(t, n_pages, pps, layer_off),
            out_specs=pl.BlockSpec((1, t, row), lambda bi, p, ptr: (bi, 0, 0)),
            scratch_shapes=[pltpu.VMEM((H * t, 1), F32), pltpu.VMEM((H * t, DH), F32)]),
        out_shape=jax.ShapeDtypeStruct((b, t, row), F32),
        compiler_params=_cparams(2, 40),
        name="sb_sample",
    )(pt, q3, kn4, vn4, *([kpool] * pps), *([vpool] * pps))


def _pad_rows(a, rows):
    return jnp.pad(a, ((0, 0), (0, rows - a.shape[1]), (0, 0)))


def _new_page(rows3):
    return _pad_rows(rows3, PAGE).reshape(rows3.shape[0], PAGE, H, DH)


def _pool_pages(cache):
    return cache.reshape((-1,) + cache.shape[2:])


def _mod_arrays(mod, idx, t, per_token):
    m = mod[:, idx]
    if per_token:
        return jnp.repeat(m, t, axis=0)[None]
    return m[:, None, :]


ROW_TILE = 512
COL_TILE = 512
GATE_TILE = 512
FOX_TILES = (512, 512)
DIFF_TILE = 512
SB_TILES = (512, 256)


def _tiles(b, t, per_token):
    return min(ROW_TILE, b * t if per_token else t), COL_TILE


def _even_layer(x2, b, t, mod, wts, i, past):
    per_token = past is not None
    tm, tn = _tiles(b, t, per_token)
    proj = norm_mod_matmul(x2, wts['norm_g'][2 * i, 0], _mod_arrays(mod, 1, t, per_token),
                           _mod_arrays(mod, 0, t, per_token), wts['w_in_even'][i], t, tm, tn)
    proj3 = proj.reshape(b, t, EVEN_N)
    if past is None:
        conv_buf = jnp.zeros((b, SC_W - 1, A_CONV), F32)
        s0 = jnp.zeros((b, H, DH, DH), F32)
    else:
        conv_buf = past['state_conv_delta'][i]
        s0 = past['state_delta'][i]
    qkv3 = delta_conv(proj3, conv_buf, wts['delta_conv_w'][i], 1 if t > SUBLANES else b)
    gates, cum = even_gates(proj3, wts['gate_bias'][i], wts['gate_alog'][i], min(t, GATE_TILE))
    t_pad = -(-t // DELTA_CHUNK) * DELTA_CHUNK
    delta_fn = delta_rule_one_chunk if t_pad == DELTA_CHUNK else delta_rule
    o_a, s_new = delta_fn(_pad_rows(qkv3, t_pad), _pad_rows(gates, t_pad), s0)
    o_a = onorm_gate(o_a[:, :t].reshape(b * t, H * DH), proj, wts['delta_onorm_g'][i], tm)
    v3 = proj3[:, :, V_B:V_B + H * DH]
    if past is None:
        q4, = head_major(proj, Q_B, wts['fox_qn_g'][i], tm, b, t, False)
        k4, kn = head_major(proj, K_B, wts['fox_kn_g'][i], tm, b, t, True)
        kn3 = kn.reshape(b, t, H * DH)
        v4, = head_major(proj, V_B, None, tm, b, t, False)
        cum_t = jnp.swapaxes(cum[:, :, LOGF_L:LOGF_L + H], 1, 2).reshape(b, H, 1, t)
        o_b = fox_prompt(q4, k4, v4, cum, cum_t, min(t, FOX_TILES[0]), min(t, FOX_TILES[1]))
    else:
        qn3 = head_norm(proj, Q_B, H * DH, wts['fox_qn_g'][i], tm).reshape(b, t, H * DH)
        kn3 = head_norm(proj, K_B, H * DH, wts['fox_kn_g'][i], tm).reshape(b, t, H * DH)
        pt = past['page_table']
        n_pool = past['cache_k_fox'].shape[1]
        cum_new = cum[:, :, LOGF_L:LOGF_L + H]
        cumn_flat = _pad_rows(cum_new, PAGE).reshape(b, 1, PAGE * H)
        cumn_col = jnp.swapaxes(cum_new, 1, 2).reshape(b, H * t, 1)
        suf_flat, tot_flat = past['fox_page_sums']
        o_b = fox_sample(pt, qn3, _new_page(kn3), _new_page(v3), cumn_flat, cumn_col,
                         _pool_pages(past['cache_k_fox']), _pool_pages(past['cache_v_fox']),
                         suf_flat, tot_flat, i * n_pool)
    m_out = (o_a, o_b.reshape(b * t, H * DH))
    rows = (kn3.reshape(b, t, H, DH), v3.reshape(b, t, H, DH), gates[:, :, LOGF_L:LOGF_L + H], s_new,
            proj3[:, t - (SC_W - 1):, :A_CONV])
    return m_out, rows


def _odd_layer(x2, b, t, mod, wts, i, l, past):
    per_token = past is not None
    tm, tn = _tiles(b, t, per_token)
    row = H * DH
    proj = norm_mod_matmul(x2, wts['norm_g'][l, 0], _mod_arrays(mod, 1, t, per_token),
                           _mod_arrays(mod, 0, t, per_token), wts['w_in_odd'][i], t, tm, tn)
    proj3 = proj.reshape(b, t, 6 * row)
    lam_init = 0.8 - 0.6 * math.exp(-0.3 * l)
    vc3 = proj3[:, :, 2 * row:3 * row]
    kd3 = proj3[:, :, 4 * row:5 * row]
    vd3 = proj3[:, :, 5 * row:6 * row]
    if past is None:
        tb = min(t, DIFF_TILE)
        tiles = t5_tiles(wts['rel_bias'], tb)
        qc4, = head_major(proj, 0, wts['diff_qn_g'][i], tm, b, t, False)
        kc4, kcn = head_major(proj, row, wts['diff_kn_g'][i], tm, b, t, True)
        kcn3 = kcn.reshape(b, t, row)
        vc4, = head_major(proj, 2 * row, None, tm, b, t, False)
        o_c = diff_prompt(qc4, kc4, vc4, tiles, wts['rel_bias'], wts['diff_lambda'][i],
                          wts['diff_subln_g'][i], lam_init, tb)
        qd4, kd4, vd4 = (head_major(proj, c * row, None, tm, b, t, False)[0] for c in (3, 4, 5))
        o_d = sb_prompt(qd4, kd4, vd4, min(t, SB_TILES[0]), min(t, SB_TILES[1]))
    else:
        qcn3 = head_norm(proj, 0, row, wts['diff_qn_g'][i], tm).reshape(b, t, row)
        kcn3 = head_norm(proj, row, row, wts['diff_kn_g'][i], tm).reshape(b, t, row)
        pt = past['page_table']
        n_pool = past['cache_k_diff'].shape[1]
        o_c = diff_sample(pt, wts['rel_bias'], qcn3, _new_page(kcn3), _new_page(vc3),
                          _pool_pages(past['cache_k_diff']), _pool_pages(past['cache_v_diff']),
                          wts['diff_lambda'][i], wts['diff_subln_g'][i], lam_init, i * n_pool)
        o_d = sb_sample(pt, proj3[:, :, 3 * row:4 * row], _new_page(kd3), _new_page(vd3),
                        _pool_pages(past['cache_k_sb']), _pool_pages(past['cache_v_sb']), i * n_pool)
    m_out = (o_c.reshape(b * t, row), o_d.reshape(b * t, row))
    rows = (kcn3.reshape(b, t, H, DH), vc3.reshape(b, t, H, DH), kd3.reshape(b, t, H, DH),
            vd3.reshape(b, t, H, DH))
    return m_out, rows


def _trunk(x, mods, wts, past):
    b, t, d = x.shape
    depth = wts['ffn_w_up'].shape[0]
    x2 = x.reshape(b * t, d)
    per_token = past is not None
    tm, tn = _tiles(b, t, per_token)
    even_rows, odd_rows, ffn_rows = [], [], []
    for l in range(depth):
        i = l // 2
        mod = mods[l]
        if l % 2 == 0:
            (a1, a2), rows = _even_layer(x2, b, t, mod, wts, i, past)
            even_rows.append(rows)
            w_out = wts['w_out_even'][i]
        else:
            (a1, a2), rows = _odd_layer(x2, b, t, mod, wts, i, l, past)
            odd_rows.append(rows)
            w_out = wts['w_out_odd'][i]
        x2 = matmul_gated_residual([a1, a2], w_out, x2, _mod_arrays(mod, 2, t, per_token), t, tm, tn)
        u = norm_mod_matmul(x2, wts['norm_g'][l, 1], _mod_arrays(mod, 4, t, per_token),
                            _mod_arrays(mod, 3, t, per_token), wts['ffn_w_up'][l], t, tm, tn)
        u3 = u.reshape(b, t, -1)
        if past is None:
            ffn_buf = jnp.zeros((b, FFN_CONV_W - 1, u3.shape[2]), F32)
        else:
            ffn_buf = past['state_conv_ffn'][l]
        act = ffn_act(u3, ffn_buf, wts['ffn_conv_w'][l], wts['ffn_conv_b'][l], 1 if t > SUBLANES else b, 512)
        ffn_rows.append(u3[:, t - (FFN_CONV_W - 1):])
        x2 = matmul_gated_residual([act.reshape(b * t, -1)], wts['ffn_w_down'][l], x2,
                                   _mod_arrays(mod, 5, t, per_token), t, tm, tn)
    stack = lambda rows_list, k: jnp.stack([r[k] for r in rows_list])
    state = [stack(even_rows, k) for k in range(5)] + [stack(odd_rows, k) for k in range(4)] + [jnp.stack(ffn_rows)]
    return x2.reshape(b, t, d), state


def _permute_even_in(w):
    row = H * DH
    c_small = A_CONV + row
    c_qb = c_small + 2 * H
    c_fb = c_qb + 3 * row
    parts = [w[..., :c_small], w[..., c_qb:c_fb], w[..., c_small:c_qb], w[..., c_fb:c_fb + H]]
    pad = EVEN_N - sum(p.shape[-1] for p in parts)
    return jnp.pad(jnp.concatenate(parts, axis=-1), ((0, 0), (0, 0), (0, pad)))


def kernel(x_prompt, x_sample, cache_k_fox, cache_v_fox, cache_logf_fox, cache_k_diff, cache_v_diff, cache_k_sb, cache_v_sb, state_delta, state_conv_delta, state_conv_ffn, page_table, c_prompt, c_sample, w_ada, b_ada, norm_g, w_in_even, w_out_even, delta_conv_w, delta_a_log, delta_dt_bias, delta_onorm_g, fox_qn_g, fox_kn_g, fox_f_bias, w_in_odd, w_out_odd, diff_qn_g, diff_kn_g, diff_lambda, diff_subln_g, rel_bias, ffn_w_up, ffn_conv_w, ffn_conv_b, ffn_w_down):
    bp, d = c_prompt.shape
    bs = c_sample.shape[0]
    n_even = w_in_even.shape[0]
    zeros8 = jnp.zeros((n_even, H), F32)
    lane_pad = jnp.zeros((n_even, LANES - 3 * H), F32)
    wts = {
        'norm_g': norm_g,
        'w_in_even': _permute_even_in(w_in_even).astype(BF16),
        'w_out_even': w_out_even.astype(BF16),
        'delta_conv_w': delta_conv_w,
        'gate_bias': jnp.concatenate([zeros8, delta_dt_bias, fox_f_bias, lane_pad], axis=-1)[:, None, :],
        'gate_alog': jnp.concatenate([zeros8, delta_a_log, zeros8, lane_pad], axis=-1)[:, None, :],
        'delta_onorm_g': delta_onorm_g, 'fox_qn_g': fox_qn_g, 'fox_kn_g': fox_kn_g,
        'w_in_odd': w_in_odd.astype(BF16), 'w_out_odd': w_out_odd.astype(BF16),
        'diff_qn_g': diff_qn_g, 'diff_kn_g': diff_kn_g, 'diff_lambda': diff_lambda,
        'diff_subln_g': diff_subln_g, 'rel_bias': rel_bias,
        'ffn_w_up': ffn_w_up.astype(BF16), 'ffn_conv_w': ffn_conv_w, 'ffn_conv_b': ffn_conv_b,
        'ffn_w_down': ffn_w_down.astype(BF16),
    }
    logf_rows = jnp.swapaxes(cache_logf_fox, 2, 3).reshape(-1, PAGE)
    page_sums = page_suffix(logf_rows, math.gcd(logf_rows.shape[0], 2048))
    to_cols = lambda a: jnp.swapaxes(a.reshape(-1, H, PAGE), 1, 2).reshape(-1, 1, PAGE * H)
    past = {
        'cache_k_fox': cache_k_fox, 'cache_v_fox': cache_v_fox,
        'fox_page_sums': (to_cols(page_sums[0]), to_cols(page_sums[1])),
        'cache_k_diff': cache_k_diff, 'cache_v_diff': cache_v_diff, 'cache_k_sb': cache_k_sb,
        'cache_v_sb': cache_v_sb, 'state_delta': state_delta, 'state_conv_delta': state_conv_delta,
        'state_conv_ffn': state_conv_ffn, 'page_table': page_table,
    }
    n_c = bp + bs
    c_all = _pad_rows(jnp.concatenate([c_prompt, c_sample], axis=0)[None], -(-n_c // SUBLANES) * SUBLANES)[0]
    mod_all = ada_mod(c_all, w_ada, b_ada)
    mods_p = [mod_all[l, :bp].reshape(bp, 6, d) for l in range(mod_all.shape[0])]
    mods_s = [mod_all[l, bp:n_c].reshape(bs, 6, d) for l in range(mod_all.shape[0])]
    y_prompt, sp = _trunk(x_prompt, mods_p, wts, None)
    y_sample, ss = _trunk(x_sample, mods_s, wts, past)
    return (y_prompt, y_sample, *sp, *ss)
```
